```python
import math
import jax, jax.numpy as jnp
from jax import lax
import numpy as np

D_MODEL = 1024
BATCH = 8
SEQ = 2048
DEPTH = 1

PLE_DIM = 256
Q_BLOCK = 128

NSA_HEADS = 8
NSA_GROUPS = 2
NSA_HPG = NSA_HEADS // NSA_GROUPS
HEAD_DIM = 64
CMP_BLOCK = 32
CMP_STRIDE = 16
CMP_HIDDEN = 256
SEL_BLOCK = 64
SEL_TOPK = 8
WINDOW = 512
SEL_FORCE = 1.0e4

DIFF_HEADS = 4
DIFF_DH = 64
DIFF_VDIM = 2 * DIFF_DH

PEER_HEADS = 8
PEER_NKEYS = 128
PEER_NEXPERTS = PEER_NKEYS * PEER_NKEYS
PEER_DKEY = 256
PEER_TOPK = 16
PEER_CHUNK = 128

NSA_WIDTH = NSA_HEADS * HEAD_DIM
NSA_KV = NSA_GROUPS * HEAD_DIM
DIFF_QK = DIFF_HEADS * DIFF_DH
DIFF_WIDTH = DIFF_HEADS * DIFF_VDIM
IN_SPLITS = (NSA_WIDTH, 6 * NSA_KV, 3 * NSA_HEADS, 2 * DIFF_QK, 2 * DIFF_QK, DIFF_WIDTH, D_MODEL, D_MODEL)
IN_COLS = NSA_WIDTH + 6 * NSA_KV + 3 * NSA_HEADS + 4 * DIFF_QK + DIFF_WIDTH + 2 * D_MODEL

DEEPNORM_ALPHA = (2.0 * DEPTH) ** 0.25
DEEPNORM_BETA = (8.0 * DEPTH) ** -0.25
NEG = -1.0e30

kernel_name = 'hybrid_nsa_diffattn_peer_block'


def alibi_slopes(n):
    return jnp.exp2(-8.0 * jnp.arange(1, n + 1, dtype=jnp.float32) / n)


def layer_norm(x, g, b, eps=1e-5):
    xf = x.astype(jnp.float32)
    mu = jnp.mean(xf, axis=-1, keepdims=True)
    var = jnp.mean(jnp.square(xf - mu), axis=-1, keepdims=True)
    return ((xf - mu) * lax.rsqrt(var + eps) * g.astype(jnp.float32) + b.astype(jnp.float32)).astype(x.dtype)


def masked_softmax(s, mask):
    s = jnp.where(mask, s, NEG)
    m = jnp.max(s, axis=-1, keepdims=True)
    e = jnp.where(mask, jnp.exp(s - m), 0.0)
    return e / jnp.maximum(jnp.sum(e, axis=-1, keepdims=True), 1e-30)


def compress_kv(kv, pos_emb, w1, w2):
    S = kv.shape[2]
    nc = (S - CMP_BLOCK) // CMP_STRIDE + 1
    idx = jnp.arange(nc)[:, None] * CMP_STRIDE + jnp.arange(CMP_BLOCK)[None, :]
    blocks = kv[:, :, idx] + pos_emb
    flat = blocks.reshape(blocks.shape[:3] + (CMP_BLOCK * HEAD_DIM,))
    return jax.nn.gelu(flat @ w1, approximate=False) @ w2


def cmp_to_sel_weights(nc, n_sel):
    c0 = jnp.arange(nc)[:, None] * CMP_STRIDE
    s0 = jnp.arange(n_sel)[None, :] * SEL_BLOCK
    ov = jnp.minimum(c0 + CMP_BLOCK, s0 + SEL_BLOCK) - jnp.maximum(c0, s0)
    return jnp.clip(ov, 0, None).astype(jnp.float32) / CMP_BLOCK


def token_mixers(h, w_in, cmp_pos_k, cmp_pos_v, cmp_k_w1, cmp_k_w2, cmp_v_w1, cmp_v_w2,
                 lam_q1, lam_k1, lam_q2, lam_k2, diff_norm_g, w_branch_nsa, w_branch_diff, w_out, layer_idx):
    B, S, _ = h.shape
    dt = h.dtype
    f32 = jnp.float32
    nq = S // Q_BLOCK
    proj = h @ w_in
    cuts = np.cumsum(IN_SPLITS)[:-1].tolist()
    q_n, kv_n, g_n, q_d, k_d, v_d, gate_a, gate_b = jnp.split(proj, cuts, axis=-1)

    q_n = q_n.reshape(B, S, NSA_GROUPS, NSA_HPG, HEAD_DIM).transpose(0, 2, 3, 1, 4) * (HEAD_DIM ** -0.5)
    kv_n = kv_n.reshape(B, S, 6, NSA_GROUPS, HEAD_DIM).transpose(2, 0, 3, 1, 4)
    k_c, v_c, k_s, v_s, k_w, v_w = [kv_n[j] for j in range(6)]
    kc = compress_kv(k_c, cmp_pos_k, cmp_k_w1, cmp_k_w2)
    vc = compress_kv(v_c, cmp_pos_v, cmp_v_w1, cmp_v_w2)
    nc = kc.shape[2]
    c_end = jnp.arange(nc) * CMP_STRIDE + (CMP_BLOCK - 1)
    n_sel = S // SEL_BLOCK
    k_sel = min(SEL_TOPK, n_sel)
    sel_w = cmp_to_sel_weights(nc, n_sel)
    ks_blk = k_s.reshape(B, NSA_GROUPS, n_sel, SEL_BLOCK, HEAD_DIM)
    vs_blk = v_s.reshape(B, NSA_GROUPS, n_sel, SEL_BLOCK, HEAD_DIM)
    kw_pad = jnp.pad(k_w, ((0, 0), (0, 0), (WINDOW, 0), (0, 0)))
    vw_pad = jnp.pad(v_w, ((0, 0), (0, 0), (WINDOW, 0), (0, 0)))
    gates_n = jax.nn.sigmoid(g_n.reshape(B, nq, Q_BLOCK, NSA_GROUPS, NSA_HPG, 3)).transpose(1, 0, 3, 4, 2, 5)
    slopes_n = alibi_slopes(NSA_HEADS).reshape(NSA_GROUPS, NSA_HPG)[None, :, :, None, None]
    q_n_blk = jnp.moveaxis(q_n.reshape(B, NSA_GROUPS, NSA_HPG, nq, Q_BLOCK, HEAD_DIM), 3, 0)
    bi = jnp.arange(B)[:, None, None, None]
    gi = jnp.arange(NSA_GROUPS)[None, :, None, None]

    q_d = q_d.reshape(B, S, 2, DIFF_HEADS, DIFF_DH) * (DIFF_DH ** -0.5)
    k_d = k_d.reshape(B, S, 2, DIFF_HEADS, DIFF_DH)
    q1_blk = jnp.moveaxis(q_d[:, :, 0].transpose(0, 2, 1, 3).reshape(B, DIFF_HEADS, nq, Q_BLOCK, DIFF_DH), 2, 0)
    q2_blk = jnp.moveaxis(q_d[:, :, 1].transpose(0, 2, 1, 3).reshape(B, DIFF_HEADS, nq, Q_BLOCK, DIFF_DH), 2, 0)
    k1 = k_d[:, :, 0].transpose(0, 2, 1, 3)
    k2 = k_d[:, :, 1].transpose(0, 2, 1, 3)
    v_d = v_d.reshape(B, S, DIFF_HEADS, DIFF_VDIM).transpose(0, 2, 1, 3)
    lam_init = 0.8 - 0.6 * math.exp(-0.3 * layer_idx)
    lam = (jnp.exp(jnp.sum(lam_q1.astype(f32) * lam_k1.astype(f32)))
           - jnp.exp(jnp.sum(lam_q2.astype(f32) * lam_k2.astype(f32))) + lam_init)
    slopes_d = alibi_slopes(DIFF_HEADS)[None, :, None, None]
    key_pos = jnp.arange(S)
    norm_g = diff_norm_g.astype(f32)

    def block(args):
        i, qn, gn, qa, qb = args
        t = i * Q_BLOCK + jnp.arange(Q_BLOCK)
        dist_c = (t[:, None] - c_end[None, :]).astype(f32)
        s = jnp.einsum('bghqd,bgcd->bghqc', qn, kc).astype(f32) - slopes_n * dist_c
        p_cmp = masked_softmax(s, dist_c >= 0)
        o_cmp = jnp.einsum('bghqc,bgcd->bghqd', p_cmp.astype(dt), vc)
        imp = jnp.einsum('bghqc,cj->bgqj', p_cmp, sel_w)
        jb = jnp.arange(n_sel)[None, :]
        cur = (t // SEL_BLOCK)[:, None]
        allowed = jb * SEL_BLOCK <= t[:, None]
        forced = (jb == 0) | (jb == cur) | (jb == cur - 1)
        score = jnp.where(forced, SEL_FORCE, jnp.where(allowed, imp, -SEL_FORCE))
        _, sel = lax.top_k(score, k_sel)
        L = k_sel * SEL_BLOCK
        ks = ks_blk[bi, gi, sel].reshape(B, NSA_GROUPS, Q_BLOCK, L, HEAD_DIM)
        vs = vs_blk[bi, gi, sel].reshape(B, NSA_GROUPS, Q_BLOCK, L, HEAD_DIM)
        pos = (sel[..., None] * SEL_BLOCK + jnp.arange(SEL_BLOCK)).reshape(B, NSA_GROUPS, Q_BLOCK, L)
        dist_s = (t[None, None, :, None] - pos).astype(f32)[:, :, None]
        s = jnp.einsum('bghqd,bgqld->bghql', qn, ks).astype(f32) - slopes_n * dist_s
        o_slc = jnp.einsum('bghql,bgqld->bghqd', masked_softmax(s, dist_s >= 0).astype(dt), vs)
        kw = lax.dynamic_slice_in_dim(kw_pad, i * Q_BLOCK, WINDOW + Q_BLOCK, axis=2)
        vw = lax.dynamic_slice_in_dim(vw_pad, i * Q_BLOCK, WINDOW + Q_BLOCK, axis=2)
        wpos = i * Q_BLOCK - WINDOW + jnp.arange(WINDOW + Q_BLOCK)
        dist_w = t[:, None] - wpos[None, :]
        mask_w = (dist_w >= 0) & (dist_w < WINDOW) & (wpos[None, :] >= 0)
        s = jnp.einsum('bghqd,bgld->bghql', qn, kw).astype(f32) - slopes_n * dist_w.astype(f32)
        o_win = jnp.einsum('bghql,bgld->bghqd', masked_softmax(s, mask_w).astype(dt), vw)
        o_nsa = gn[..., 0:1] * o_cmp + gn[..., 1:2] * o_slc + gn[..., 2:3] * o_win
        o_nsa = o_nsa.transpose(0, 3, 1, 2, 4).reshape(B, Q_BLOCK, NSA_WIDTH)
        dist_d = (t[:, None] - key_pos[None, :]).astype(f32)
        mask_d = dist_d >= 0
        bias = -slopes_d * dist_d
        a1 = masked_softmax(jnp.einsum('bhqd,bhkd->bhqk', qa, k1).astype(f32) + bias, mask_d)
        a2 = masked_softmax(jnp.einsum('bhqd,bhkd->bhqk', qb, k2).astype(f32) + bias, mask_d)
        o = jnp.einsum('bhqk,bhkv->bhqv', (a1 - lam * a2).astype(dt), v_d).astype(f32)
        o = o * lax.rsqrt(jnp.mean(o * o, axis=-1, keepdims=True) + 1e-5) * norm_g * (1.0 - lam_init)
        o_diff = o.astype(dt).transpose(0, 2, 1, 3).reshape(B, Q_BLOCK, DIFF_WIDTH)
        return o_nsa, o_diff

    o_nsa, o_diff = lax.map(block, (jnp.arange(nq), q_n_blk, gates_n, q1_blk, q2_blk))
    o_nsa = jnp.moveaxis(o_nsa, 0, 1).reshape(B, S, NSA_WIDTH)
    o_diff = jnp.moveaxis(o_diff, 0, 1).reshape(B, S, DIFF_WIDTH)
    merged = (jax.nn.sigmoid(gate_a) * (o_nsa @ w_branch_nsa)
              + jax.nn.sigmoid(gate_b) * (o_diff @ w_branch_diff))
    return merged @ w_out


def peer_ffn(h, wq, subkeys1, subkeys2, u_tab, v_tab):
    B, S, D = h.shape
    f32 = jnp.float32
    q = (h @ wq).reshape(B, S, PEER_HEADS, 2, PEER_DKEY // 2)
    s1 = jnp.einsum('bshc,kc->bshk', q[:, :, :, 0], subkeys1).astype(f32)
    s2 = jnp.einsum('bshc,kc->bshk', q[:, :, :, 1], subkeys2).astype(f32)
    v1, i1 = lax.top_k(s1, PEER_TOPK)
    v2, i2 = lax.top_k(s2, PEER_TOPK)
    cand = (v1[..., :, None] + v2[..., None, :]).reshape(B, S, PEER_HEADS, PEER_TOPK * PEER_TOPK)
    cid = (i1[..., :, None] * PEER_NKEYS + i2[..., None, :]).reshape(B, S, PEER_HEADS, PEER_TOPK * PEER_TOPK)
    sc, pick = lax.top_k(cand, PEER_TOPK)
    eid = jnp.take_along_axis(cid, pick, axis=-1)
    g = jax.nn.softmax(sc, axis=-1).astype(h.dtype)
    n_chunk = (B * S) // PEER_CHUNK
    xs = h.reshape(n_chunk, PEER_CHUNK, D)
    es = eid.reshape(n_chunk, PEER_CHUNK, PEER_HEADS, PEER_TOPK)
    gs = g.reshape(n_chunk, PEER_CHUNK, PEER_HEADS, PEER_TOPK)

    def chunk(args):
        xc, ec, gc = args
        u = u_tab[ec]
        v = v_tab[ec]
        a = jax.nn.gelu(jnp.einsum('cd,chkd->chk', xc, u), approximate=False)
        return jnp.einsum('chk,chkd->cd', gc * a, v)

    return lax.map(chunk, (xs, es, gs)).reshape(B, S, D)


def setup_inputs(seed: int = 0) -> dict:
    key = jax.random.key(seed)
    ks = jax.random.split(key, 32)
    f32 = jnp.float32
    L = DEPTH

    def nrm(k, shape, scale):
        return jax.random.normal(k, shape, f32) * scale

    return {
        'x': nrm(ks[0], (BATCH, SEQ, D_MODEL), 1.0),
        'p': nrm(ks[1], (DEPTH, BATCH, SEQ, PLE_DIM), 1.0),
        'w_in': nrm(ks[2], (L, D_MODEL, IN_COLS), D_MODEL ** -0.5),
        'cmp_pos_k': nrm(ks[3], (L, CMP_BLOCK, HEAD_DIM), 0.1),
        'cmp_pos_v': nrm(ks[4], (L, CMP_BLOCK, HEAD_DIM), 0.1),
        'cmp_k_w1': nrm(ks[5], (L, CMP_BLOCK * HEAD_DIM, CMP_HIDDEN), (CMP_BLOCK * HEAD_DIM) ** -0.5),
        'cmp_k_w2': nrm(ks[6], (L, CMP_HIDDEN, HEAD_DIM), CMP_HIDDEN ** -0.5),
        'cmp_v_w1': nrm(ks[7], (L, CMP_BLOCK * HEAD_DIM, CMP_HIDDEN), (CMP_BLOCK * HEAD_DIM) ** -0.5),
        'cmp_v_w2': nrm(ks[8], (L, CMP_HIDDEN, HEAD_DIM), CMP_HIDDEN ** -0.5),
        'lam_q1': nrm(ks[9], (L, DIFF_DH), 0.1),
        'lam_k1': nrm(ks[10], (L, DIFF_DH), 0.1),
        'lam_q2': nrm(ks[11], (L, DIFF_DH), 0.1),
        'lam_k2': nrm(ks[12], (L, DIFF_DH), 0.1),
        'diff_norm_g': 1.0 + nrm(ks[13], (L, DIFF_VDIM), 0.02),
        'w_branch_nsa': nrm(ks[14], (L, NSA_WIDTH, D_MODEL), NSA_WIDTH ** -0.5 * DEEPNORM_BETA),
        'w_branch_diff': nrm(ks[15], (L, DIFF_WIDTH, D_MODEL), DIFF_WIDTH ** -0.5 * DEEPNORM_BETA),
        'w_out': nrm(ks[16], (L, D_MODEL, D_MODEL), D_MODEL ** -0.5 * DEEPNORM_BETA),
        'ln1_g': 1.0 + nrm(ks[17], (L, D_MODEL), 0.02),
        'ln1_b': nrm(ks[18], (L, D_MODEL), 0.02),
        'peer_wq': nrm(ks[19], (L, D_MODEL, PEER_HEADS * PEER_DKEY), D_MODEL ** -0.5),
        'peer_subkeys1': nrm(ks[20], (L, PEER_NKEYS, PEER_DKEY // 2), (PEER_DKEY // 2) ** -0.5),
        'peer_subkeys2': nrm(ks[21], (L, PEER_NKEYS, PEER_DKEY // 2), (PEER_DKEY // 2) ** -0.5),
        'peer_u': nrm(ks[22], (L, PEER_NEXPERTS, D_MODEL), D_MODEL ** -0.5),
        'peer_v': nrm(ks[23], (L, PEER_NEXPERTS, D_MODEL), DEEPNORM_BETA * PEER_HEADS ** -0.5),
        'ln2_g': 1.0 + nrm(ks[24], (L, D_MODEL), 0.02),
        'ln2_b': nrm(ks[25], (L, D_MODEL), 0.02),
        'ple_w_proj': nrm(ks[26], (L, PLE_DIM, D_MODEL), PLE_DIM ** -0.5 * DEEPNORM_BETA),
        'ple_w_gate': nrm(ks[27], (L, D_MODEL, D_MODEL), D_MODEL ** -0.5),
    }


def reference(x, p, w_in, cmp_pos_k, cmp_pos_v, cmp_k_w1, cmp_k_w2, cmp_v_w1, cmp_v_w2,
              lam_q1, lam_k1, lam_q2, lam_k2, diff_norm_g, w_branch_nsa, w_branch_diff, w_out,
              ln1_g, ln1_b, peer_wq, peer_subkeys1, peer_subkeys2, peer_u, peer_v, ln2_g, ln2_b,
              ple_w_proj, ple_w_gate):
    h = x
    for i in range(DEPTH):
        mix = token_mixers(h, w_in[i], cmp_pos_k[i], cmp_pos_v[i], cmp_k_w1[i], cmp_k_w2[i],
                           cmp_v_w1[i], cmp_v_w2[i], lam_q1[i], lam_k1[i], lam_q2[i], lam_k2[i],
                           diff_norm_g[i], w_branch_nsa[i], w_branch_diff[i], w_out[i], i)
        h = layer_norm(DEEPNORM_ALPHA * h + mix, ln1_g[i], ln1_b[i])
        ffn = peer_ffn(h, peer_wq[i], peer_subkeys1[i], peer_subkeys2[i], peer_u[i], peer_v[i])
        h = layer_norm(DEEPNORM_ALPHA * h + ffn, ln2_g[i], ln2_b[i])
        h = h + jax.nn.sigmoid(h @ ple_w_gate[i]) * (p[i] @ ple_w_proj[i])
    return h
```

```python
import functools
import math

import jax
import jax.numpy as jnp
from jax import lax
from jax.experimental import pallas as pl
from jax.experimental.pallas import tpu as pltpu

D_MODEL = 1024
DEPTH = 1
PLE_DIM = 256
Q_BLOCK = 128

NSA_HEADS = 8
NSA_GROUPS = 2
NSA_HPG = NSA_HEADS // NSA_GROUPS
HEAD_DIM = 64
CMP_BLOCK = 32
CMP_STRIDE = 16
CMP_HIDDEN = 256
SEL_BLOCK = 64
SEL_TOPK = 8
WINDOW = 512
SEL_FORCE = 1.0e4

DIFF_HEADS = 4
DIFF_DH = 64
DIFF_VDIM = 2 * DIFF_DH

PEER_HEADS = 8
PEER_NKEYS = 128
PEER_NEXPERTS = PEER_NKEYS * PEER_NKEYS
PEER_DKEY = 256
PEER_TOPK = 16

NSA_WIDTH = NSA_HEADS * HEAD_DIM
NSA_KV = NSA_GROUPS * HEAD_DIM
DIFF_QK = DIFF_HEADS * DIFF_DH
DIFF_WIDTH = DIFF_HEADS * DIFF_VDIM

DEEPNORM_ALPHA = (2.0 * DEPTH) ** 0.25
NEG = -1.0e30
LN_EPS = 1e-5

BF = jnp.bfloat16
F32 = jnp.float32
I32 = jnp.int32

LANE = 128
KEY_CHUNK = 256
N_CMP = 128
N_SEL = 32
VMEM_LIMIT = 56 * 1024 * 1024

C_QN = 0
C_KV = C_QN + NSA_WIDTH
C_G = C_KV + 6 * NSA_KV
C_QD = C_G + NSA_GROUPS * LANE
C_KD = C_QD + 2 * DIFF_QK
C_VD = C_KD + 2 * DIFF_QK
C_GA = C_VD + DIFF_WIDTH
C_GB = C_GA + D_MODEL
C_END = C_GB + D_MODEL


def _dot(a, b):
    return jnp.dot(a, b, preferred_element_type=F32)


def _dot_nt(a, b):
    return lax.dot_general(a, b, (((1,), (1,)), ((), ())), preferred_element_type=F32)


def _gelu(x):
    return 0.5 * x * (1.0 + lax.erf(x * (2.0 ** -0.5)))


def _layer_norm(y, g, b):
    mu = jnp.mean(y, axis=-1, keepdims=True)
    d = y - mu
    var = jnp.mean(d * d, axis=-1, keepdims=True)
    return d * lax.rsqrt(var + LN_EPS) * g + b


def _proj_kernel(x_ref, w_ref, qn_ref, kv_ref, g_ref, qd_ref, kd_ref, vd_ref, ga_ref, gb_ref):
    x = x_ref[...].astype(BF)

    def seg(c0, c1):
        return _dot(x, w_ref[:, c0:c1])

    qn_ref[...] = seg(C_QN, C_KV).astype(BF)
    kv_ref[...] = seg(C_KV, C_G).astype(BF)
    g_ref[...] = jax.nn.sigmoid(seg(C_G, C_QD))
    qd_ref[...] = seg(C_QD, C_KD).astype(BF)
    kd_ref[...] = seg(C_KD, C_VD).astype(BF)
    vd_ref[...] = seg(C_VD, C_GA).astype(BF)
    ga_ref[...] = jax.nn.sigmoid(seg(C_GA, C_GB)).astype(BF)
    gb_ref[...] = jax.nn.sigmoid(seg(C_GB, C_END)).astype(BF)


def _proj(x2, w_all, tm=512):
    n = x2.shape[0]
    widths = [(C_KV - C_QN, BF), (C_G - C_KV, BF), (C_QD - C_G, F32), (C_KD - C_QD, BF),
              (C_VD - C_KD, BF), (C_GA - C_VD, BF), (C_GB - C_GA, BF), (C_END - C_GB, BF)]
    return pl.pallas_call(
        _proj_kernel,
        grid=(n // tm,),
        in_specs=[pl.BlockSpec((tm, D_MODEL), lambda i: (i, 0)),
                  pl.BlockSpec((D_MODEL, C_END), lambda i: (0, 0))],
        out_specs=[pl.BlockSpec((tm, w), lambda i: (i, 0)) for w, _ in widths],
        out_shape=[jax.ShapeDtypeStruct((n, w), dt) for w, dt in widths],
        compiler_params=pltpu.CompilerParams(dimension_semantics=("arbitrary",),
                                             vmem_limit_bytes=VMEM_LIMIT),
        name="proj",
    )(x2, w_all)


def _compress_kernel(r_ref, posk_ref, posv_ref, wk1_ref, wk2_ref, wv1_ref, wv2_ref, o_ref):
    half = CMP_STRIDE * HEAD_DIM
    for idx, (pos_ref, w1_ref, w2_ref) in enumerate(((posk_ref, wk1_ref, wk2_ref),
                                                     (posv_ref, wv1_ref, wv2_ref))):
        r = r_ref[0, 0, idx].astype(F32)
        ra = (r + pos_ref[0:1, :]).astype(BF)
        rb = (r + pos_ref[1:2, :]).astype(BF)
        p1 = _dot(ra, w1_ref[0:half, :])
        p2 = _dot(rb, w1_ref[half:2 * half, :])
        hid = p1 + pltpu.roll(p2, N_CMP - 1, 0)
        c = _dot(_gelu(hid).astype(BF), w2_ref[...])
        o_ref[0, 0, :, idx * HEAD_DIM:(idx + 1) * HEAD_DIM] = c.astype(BF)


def _compress(r, posk, posv, wk1, wk2, wv1, wv2):
    b = r.shape[0]
    full = lambda a: pl.BlockSpec(a.shape, lambda i, j: (0,) * a.ndim)
    return pl.pallas_call(
        _compress_kernel,
        grid=(b, NSA_GROUPS),
        in_specs=[pl.BlockSpec((1, 1, 2, N_CMP, CMP_STRIDE * HEAD_DIM), lambda i, j: (i, j, 0, 0, 0)),
                  full(posk), full(posv), full(wk1), full(wk2), full(wv1), full(wv2)],
        out_specs=pl.BlockSpec((1, 1, N_CMP, 2 * HEAD_DIM), lambda i, j: (i, j, 0, 0)),
        out_shape=jax.ShapeDtypeStruct((b, NSA_GROUPS, N_CMP, 2 * HEAD_DIM), BF),
        compiler_params=pltpu.CompilerParams(dimension_semantics=("arbitrary", "arbitrary")),
        name="compress",
    )(r, posk, posv, wk1, wk2, wv1, wv2)


def _flash_step(carry, q, k, v, bias, valid):
    m, l, acc = carry
    s = jnp.where(valid, _dot_nt(q, k) - bias, NEG)
    m_new = jnp.maximum(m, jnp.max(s, axis=-1, keepdims=True))
    a = jnp.exp(m - m_new)
    p = jnp.where(valid, jnp.exp(s - m_new), 0.0)
    l = a * l + jnp.sum(p, axis=-1, keepdims=True)
    acc = a * acc + _dot(p.astype(BF), v)
    return m_new, l, acc


def _flash_init(rows, dv):
    return (jnp.full((rows, 1), NEG, F32), jnp.zeros((rows, 1), F32), jnp.zeros((rows, dv), F32))


def _flash_out(carry):
    _, l, acc = carry
    return acc / jnp.maximum(l, 1e-30)


def _nsa_kernel(q_ref, kv_ref, cmp_ref, g_ref, selw_ref, o_ref):
    grp = pl.program_id(1)
    i = pl.program_id(2)
    rows = NSA_HPG * Q_BLOCK
    q = q_ref[...]
    q4 = jnp.concatenate([q[:, h * HEAD_DIM:(h + 1) * HEAD_DIM] for h in range(NSA_HPG)], axis=0)
    row = lax.broadcasted_iota(I32, (rows, 1), 0)
    t_row = i * Q_BLOCK + (row & (Q_BLOCK - 1))
    head = grp * NSA_HPG + (row >> 7)
    slope = jnp.exp2(-(head + 1).astype(F32))

    kc = cmp_ref[0, 0, :, 0:HEAD_DIM]
    vc = cmp_ref[0, 0, :, HEAD_DIM:2 * HEAD_DIM]
    c_end = lax.broadcasted_iota(I32, (1, N_CMP), 1) * CMP_STRIDE + (CMP_BLOCK - 1)
    dist_c = (t_row - c_end).astype(F32)
    valid_c = dist_c >= 0
    s = jnp.where(valid_c, _dot_nt(q4, kc) - slope * dist_c, NEG)
    m = jnp.max(s, axis=-1, keepdims=True)
    e = jnp.where(valid_c, jnp.exp(s - m), 0.0)
    p_cmp = e / jnp.maximum(jnp.sum(e, axis=-1, keepdims=True), 1e-30)
    o_cmp = _dot(p_cmp.astype(BF), vc)

    psum = (p_cmp[0:Q_BLOCK] + p_cmp[Q_BLOCK:2 * Q_BLOCK]
            + p_cmp[2 * Q_BLOCK:3 * Q_BLOCK] + p_cmp[3 * Q_BLOCK:4 * Q_BLOCK])
    p_hi = psum.astype(BF)
    p_lo = (psum - p_hi.astype(F32)).astype(BF)
    selw = selw_ref[...]
    imp = _dot(p_hi, selw) + _dot(p_lo, selw)
    jb = lax.broadcasted_iota(I32, (Q_BLOCK, LANE), 1)
    tq = i * Q_BLOCK + lax.broadcasted_iota(I32, (Q_BLOCK, LANE), 0)
    cur = tq >> 6
    allowed = jb * SEL_BLOCK <= tq
    forced = (jb == 0) | (jb == cur) | (jb == cur - 1)
    score = jnp.where(forced, SEL_FORCE, jnp.where(allowed, imp, -SEL_FORCE))
    score = jnp.where(jb < N_SEL, score, -jnp.inf)
    sel = jnp.zeros((Q_BLOCK, LANE), F32)
    for _ in range(SEL_TOPK):
        mx = jnp.max(score, axis=-1, keepdims=True)
        idx = jnp.min(jnp.where(score == mx, jb, LANE), axis=-1, keepdims=True)
        hit = jb == idx
        sel = jnp.where(hit, 1.0, sel)
        score = jnp.where(hit, -jnp.inf, score)
    sel_b = sel.astype(BF)

    key_iota = lax.broadcasted_iota(I32, (1, KEY_CHUNK), 1)
    blk_iota = lax.broadcasted_iota(I32, (LANE, KEY_CHUNK), 0)
    key_blk = lax.broadcasted_iota(I32, (LANE, KEY_CHUNK), 1) >> 6
    n_chunk = (i >> 1) + 1

    def kv_chunk(j, c0):
        start = pl.multiple_of(j * KEY_CHUNK, KEY_CHUNK)
        return kv_ref[pl.ds(start, KEY_CHUNK), c0:c0 + HEAD_DIM]

    def sel_body(j, carry):
        expand = jnp.where(blk_iota == j * (KEY_CHUNK // SEL_BLOCK) + key_blk, 1.0, 0.0).astype(BF)
        picked = jnp.concatenate([_dot(sel_b, expand)] * NSA_HPG, axis=0) > 0.5
        dist = (t_row - (j * KEY_CHUNK + key_iota)).astype(F32)
        valid = picked & (dist >= 0)
        return _flash_step(carry, q4, kv_chunk(j, 2 * HEAD_DIM), kv_chunk(j, 3 * HEAD_DIM),
                           slope * dist, valid)

    o_slc = _flash_out(lax.fori_loop(0, n_chunk, sel_body, _flash_init(rows, HEAD_DIM)))

    def win_body(j, carry):
        dist = (t_row - (j * KEY_CHUNK + key_iota)).astype(F32)
        valid = (dist >= 0) & (dist < WINDOW)
        return _flash_step(carry, q4, kv_chunk(j, 4 * HEAD_DIM), kv_chunk(j, 5 * HEAD_DIM),
                           slope * dist, valid)

    w_start = jnp.maximum(i - WINDOW // Q_BLOCK, 0) >> 1
    o_win = _flash_out(lax.fori_loop(w_start, n_chunk, win_body, _flash_init(rows, HEAD_DIM)))

    gates = g_ref[...]
    for h in range(NSA_HPG):
        r0 = h * Q_BLOCK
        o_h = (gates[:, 3 * h:3 * h + 1] * o_cmp[r0:r0 + Q_BLOCK]
               + gates[:, 3 * h + 1:3 * h + 2] * o_slc[r0:r0 + Q_BLOCK]
               + gates[:, 3 * h + 2:3 * h + 3] * o_win[r0:r0 + Q_BLOCK])
        o_ref[:, h * HEAD_DIM:(h + 1) * HEAD_DIM] = o_h.astype(BF)


def _nsa(qn, kv, cmp, gates, selw, batch, seq):
    nq = seq // Q_BLOCK
    gw = NSA_HPG * HEAD_DIM
    return pl.pallas_call(
        _nsa_kernel,
        grid=(batch, NSA_GROUPS, nq),
        in_specs=[pl.BlockSpec((Q_BLOCK, gw), lambda b, g, i: (b * nq + i, g)),
                  pl.BlockSpec((seq, 6 * HEAD_DIM), lambda b, g, i: (b, g)),
                  pl.BlockSpec((1, 1, N_CMP, 2 * HEAD_DIM), lambda b, g, i: (b, g, 0, 0)),
                  pl.BlockSpec((Q_BLOCK, LANE), lambda b, g, i: (b * nq + i, g)),
                  pl.BlockSpec((N_CMP, LANE), lambda b, g, i: (0, 0))],
        out_specs=pl.BlockSpec((Q_BLOCK, gw), lambda b, g, i: (b * nq + i, g)),
        out_shape=jax.ShapeDtypeStruct((batch * seq, NSA_WIDTH), BF),
        compiler_params=pltpu.CompilerParams(dimension_semantics=("arbitrary",) * 3),
        name="nsa",
    )(qn, kv, cmp, gates, selw)


def _diff_kernel(q_ref, k_ref, v_ref, lam_ref, ng_ref, o_ref):
    h = pl.program_id(1)
    i = pl.program_id(2)
    q = q_ref[...]
    q1 = q[:, 0:DIFF_DH]
    q2 = q[:, DIFF_DH:2 * DIFF_DH]
    lam_init = 0.8 - 0.6 * math.exp(-0.3 * 0)
    lv = lam_ref[...]
    lam = (jnp.exp(jnp.sum(lv[0:1] * lv[1:2], axis=-1, keepdims=True))
           - jnp.exp(jnp.sum(lv[2:3] * lv[3:4], axis=-1, keepdims=True)) + lam_init)
    slope = jnp.exp2(jnp.full((1, 1), -(8 // DIFF_HEADS) * (h + 1), I32).astype(F32))
    t_row = i * Q_BLOCK + lax.broadcasted_iota(I32, (Q_BLOCK, 1), 0)
    key_iota = lax.broadcasted_iota(I32, (1, KEY_CHUNK), 1)

    def body(j, carry):
        c1, c2 = carry
        start = pl.multiple_of(j * KEY_CHUNK, KEY_CHUNK)
        kk = k_ref[pl.ds(start, KEY_CHUNK), :]
        v = v_ref[pl.ds(start, KEY_CHUNK), :]
        dist = (t_row - (j * KEY_CHUNK + key_iota)).astype(F32)
        valid = dist >= 0
        bias = slope * dist
        c1 = _flash_step(c1, q1, kk[:, 0:DIFF_DH], v, bias, valid)
        c2 = _flash_step(c2, q2, kk[:, DIFF_DH:2 * DIFF_DH], v, bias, valid)
        return c1, c2

    init = (_flash_init(Q_BLOCK, DIFF_VDIM), _flash_init(Q_BLOCK, DIFF_VDIM))
    c1, c2 = lax.fori_loop(0, (i >> 1) + 1, body, init)
    o = _flash_out(c1) - lam * _flash_out(c2)
    o = o * lax.rsqrt(jnp.mean(o * o, axis=-1, keepdims=True) + 1e-5) * ng_ref[...] * (1.0 - lam_init)
    o_ref[...] = o.astype(BF)


def _diff(qd, kd, vd, lamv, ng, batch, seq):
    nq = seq // Q_BLOCK
    return pl.pallas_call(
        _diff_kernel,
        grid=(batch, DIFF_HEADS, nq),
        in_specs=[pl.BlockSpec((Q_BLOCK, 2 * DIFF_DH), lambda b, h, i: (b * nq + i, h)),
                  pl.BlockSpec((seq, 2 * DIFF_DH), lambda b, h, i: (b, h)),
                  pl.BlockSpec((seq, DIFF_VDIM), lambda b, h, i: (b, h)),
                  pl.BlockSpec((4, DIFF_DH), lambda b, h, i: (0, 0)),
                  pl.BlockSpec((1, DIFF_VDIM), lambda b, h, i: (0, 0))],
        out_specs=pl.BlockSpec((Q_BLOCK, DIFF_VDIM), lambda b, h, i: (b * nq + i, h)),
        out_shape=jax.ShapeDtypeStruct((batch * seq, DIFF_WIDTH), BF),
        compiler_params=pltpu.CompilerParams(dimension_semantics=("arbitrary",) * 3),
        name="diff",
    )(qd, kd, vd, lamv, ng)


def _merge_kernel(on_ref, od_ref, ga_ref, gb_ref, x_ref, wn_ref, wd_ref, wo_ref, g_ref, b_ref,
                  h_ref, hb_ref):
    merged = (ga_ref[...].astype(F32) * _dot(on_ref[...], wn_ref[...])
              + gb_ref[...].astype(F32) * _dot(od_ref[...], wd_ref[...]))
    mix = _dot(merged.astype(BF), wo_ref[...])
    h = _layer_norm(DEEPNORM_ALPHA * x_ref[...] + mix, g_ref[...], b_ref[...])
    h_ref[...] = h
    hb_ref[...] = h.astype(BF)


def _merge(o_nsa, o_diff, ga, gb, x2, wn, wd, wo, g, b, tm=256):
    n = x2.shape[0]
    tok = lambda w: pl.BlockSpec((tm, w), lambda i: (i, 0))
    full = lambda a: pl.BlockSpec(a.shape, lambda i: (0, 0))
    return pl.pallas_call(
        _merge_kernel,
        grid=(n // tm,),
        in_specs=[tok(NSA_WIDTH), tok(DIFF_WIDTH), tok(D_MODEL), tok(D_MODEL), tok(D_MODEL),
                  full(wn), full(wd), full(wo), full(g), full(b)],
        out_specs=[tok(D_MODEL), tok(D_MODEL)],
        out_shape=[jax.ShapeDtypeStruct((n, D_MODEL), F32), jax.ShapeDtypeStruct((n, D_MODEL), BF)],
        compiler_params=pltpu.CompilerParams(dimension_semantics=("arbitrary",),
                                             vmem_limit_bytes=VMEM_LIMIT),
        name="merge",
    )(o_nsa, o_diff, ga, gb, x2, wn, wd, wo, g, b)


N_CAND = PEER_TOPK + 8 * (PEER_TOPK - 1)


def _extract_topk(s, k):
    nrow, t = s.shape
    rows = lax.broadcasted_iota(I32, (nrow, t), 0)
    krow = lax.broadcasted_iota(I32, (k, t), 0)

    def body(r, carry):
        s, pos, vals = carry
        m = jnp.max(s, axis=0, keepdims=True)
        idx = jnp.min(jnp.where(s == m, rows, nrow), axis=0, keepdims=True)
        hit = rows == idx
        pos = jnp.where(hit, r, pos)
        s = jnp.where(hit, -jnp.inf, s)
        vals = jnp.where(krow == r, m, vals)
        return s, pos, vals

    init = (s, jnp.full((nrow, t), k, I32), jnp.zeros((k, t), F32))
    _, pos, vals = lax.fori_loop(0, k, body, init)
    return vals, pos


def _route_kernel(h_ref, wq_ref, sk1_ref, sk2_ref, c_ref, p2_ref, e1_ref, e2_ref, q_scr):
    t = h_ref.shape[0]
    q_scr[...] = _dot_nt(wq_ref[...], h_ref[...]).astype(BF)
    half = PEER_DKEY // 2

    crow = lax.broadcasted_iota(I32, (N_CAND, t), 0)
    ci = jnp.where(crow < PEER_TOPK, 0, ((crow - PEER_TOPK) >> 3) + 1)
    cj = jnp.where(crow < PEER_TOPK, crow, (crow - PEER_TOPK) & 7)
    cand_ok = (ci + 1) * (cj + 1) <= PEER_TOPK
    krow = lax.broadcasted_iota(I32, (PEER_TOPK, t), 0)

    def head_body(hd, _):
        base = pl.multiple_of(hd * PEER_DKEY, PEER_DKEY)
        s1 = _dot(sk1_ref[...], q_scr[pl.ds(base, half), :])
        s2 = _dot(sk2_ref[...], q_scr[pl.ds(base + half, half), :])
        v1, pos1 = _extract_topk(s1, PEER_TOPK)
        v2, pos2 = _extract_topk(s2, PEER_TOPK)
        pieces = [v1[0:1] + v2]
        for r in range(1, PEER_TOPK):
            pieces.append(v1[r:r + 1] + v2[0:8])
        cand = jnp.where(cand_ok, jnp.concatenate(pieces, axis=0), -jnp.inf)
        _, cpos = _extract_topk(cand, PEER_TOPK)
        picked = cpos < PEER_TOPK
        top = v1[0:1] + v2[0:1]
        z = jnp.sum(jnp.where(picked, jnp.exp(cand - top), 0.0), axis=0, keepdims=True)
        pk = picked.astype(F32)
        cnt = jnp.zeros((PEER_TOPK, t), F32)
        cnt = jnp.where(krow == 0, jnp.sum(pk[0:PEER_TOPK], axis=0, keepdims=True), cnt)
        for r in range(1, PEER_TOPK):
            lo = PEER_TOPK + 8 * (r - 1)
            cnt = jnp.where(krow == r, jnp.sum(pk[lo:lo + 8], axis=0, keepdims=True), cnt)
        c = jnp.zeros((PEER_NKEYS, t), F32)
        for r in range(PEER_TOPK):
            c = jnp.where(pos1 == r, cnt[r:r + 1], c)
        c_ref[hd] = c
        p2_ref[hd] = pos2.astype(F32)
        e1_ref[hd] = jnp.exp(s1 - v1[0:1]) / z
        e2_ref[hd] = jnp.exp(s2 - v2[0:1])
        return 0

    lax.fori_loop(0, PEER_HEADS, head_body, 0)


def _route(hb, wq_t, sk1, sk2, tm=256):
    n = hb.shape[0]
    full = lambda a: pl.BlockSpec(a.shape, lambda i: (0, 0))
    out_spec = pl.BlockSpec((PEER_HEADS, PEER_NKEYS, tm), lambda i: (0, 0, i))
    out_shape = jax.ShapeDtypeStruct((PEER_HEADS, PEER_NKEYS, n), F32)
    return pl.pallas_call(
        _route_kernel,
        grid=(n // tm,),
        in_specs=[pl.BlockSpec((tm, D_MODEL), lambda i: (i, 0)), full(wq_t), full(sk1), full(sk2)],
        out_specs=[out_spec] * 4,
        out_shape=[out_shape] * 4,
        scratch_shapes=[pltpu.VMEM((PEER_HEADS * PEER_DKEY, tm), BF)],
        compiler_params=pltpu.CompilerParams(dimension_semantics=("arbitrary",),
                                             vmem_limit_bytes=VMEM_LIMIT),
        name="route",
    )(hb, wq_t, sk1, sk2)


def _peer_kernel(x_ref, u_ref, vt_ref, c_ref, p2_ref, e1_ref, e2_ref, o_ref):
    j = pl.program_id(1)
    te = u_ref.shape[0]
    n_a = te // PEER_NKEYS

    @pl.when(j == 0)
    def _():
        o_ref[...] = jnp.zeros_like(o_ref)

    act = _gelu(_dot_nt(u_ref[...], x_ref[...]))
    ys = []
    for al in range(n_a):
        a = j * n_a + al
        w = jnp.zeros((PEER_NKEYS, act.shape[1]), F32)
        for hd in range(PEER_HEADS):
            ca = c_ref[hd, pl.ds(a, 1), :]
            e1a = e1_ref[hd, pl.ds(a, 1), :]
            w = w + jnp.where(p2_ref[hd] < ca, e2_ref[hd] * e1a, 0.0)
        ys.append((w * act[al * PEER_NKEYS:(al + 1) * PEER_NKEYS]).astype(BF))
    y = jnp.concatenate(ys, axis=0)
    o_ref[...] += _dot(vt_ref[...], y)


def _peer(hb, u_b, vt_b, c, p2, e1, e2, tm=512, te=512):
    n = hb.shape[0]
    ne = u_b.shape[0]
    rt = pl.BlockSpec((PEER_HEADS, PEER_NKEYS, tm), lambda i, j: (0, 0, i))
    return pl.pallas_call(
        _peer_kernel,
        grid=(n // tm, ne // te),
        in_specs=[pl.BlockSpec((tm, D_MODEL), lambda i, j: (i, 0)),
                  pl.BlockSpec((te, D_MODEL), lambda i, j: (j, 0)),
                  pl.BlockSpec((D_MODEL, te), lambda i, j: (0, j)),
                  rt, rt, rt, rt],
        out_specs=pl.BlockSpec((D_MODEL, tm), lambda i, j: (0, i)),
        out_shape=jax.ShapeDtypeStruct((D_MODEL, n), F32),
        compiler_params=pltpu.CompilerParams(dimension_semantics=("arbitrary", "arbitrary"),
                                             vmem_limit_bytes=VMEM_LIMIT),
        name="peer",
    )(hb, u_b, vt_b, c, p2, e1, e2)


def _final_kernel(ft_ref, h_ref, p_ref, wg_ref, wp_ref, g_ref, b_ref, o_ref):
    ffn = ft_ref[...].T
    h = _layer_norm(DEEPNORM_ALPHA * h_ref[...] + ffn, g_ref[...], b_ref[...])
    gate = jax.nn.sigmoid(_dot(h.astype(BF), wg_ref[...]))
    o_ref[...] = h + gate * _dot(p_ref[...].astype(BF), wp_ref[...])


def _final(ffn_t, h1, p2d, wg, wp, g, b, tm=256):
    n = h1.shape[0]
    full = lambda a: pl.BlockSpec(a.shape, lambda i: (0, 0))
    return pl.pallas_call(
        _final_kernel,
        grid=(n // tm,),
        in_specs=[pl.BlockSpec((D_MODEL, tm), lambda i: (0, i)),
                  pl.BlockSpec((tm, D_MODEL), lambda i: (i, 0)),
                  pl.BlockSpec((tm, PLE_DIM), lambda i: (i, 0)),
                  full(wg), full(wp), full(g), full(b)],
        out_specs=pl.BlockSpec((tm, D_MODEL), lambda i: (i, 0)),
        out_shape=jax.ShapeDtypeStruct((n, D_MODEL), F32),
        compiler_params=pltpu.CompilerParams(dimension_semantics=("arbitrary",),
                                             vmem_limit_bytes=VMEM_LIMIT),
        name="final",
    )(ffn_t, h1, p2d, wg, wp, g, b)


def _arrange_w_in(w_in):
    cuts = [NSA_WIDTH, 6 * NSA_KV, 3 * NSA_HEADS, 2 * DIFF_QK, 2 * DIFF_QK, DIFF_WIDTH, D_MODEL, D_MODEL]
    offs = [0]
    for c in cuts:
        offs.append(offs[-1] + c)
    w_qn, w_kv, w_g, w_qd, w_kd, w_vd, w_ga, w_gb = [w_in[:, offs[k]:offs[k + 1]] for k in range(8)]
    d = w_in.shape[0]
    w_qn = w_qn * (HEAD_DIM ** -0.5)
    w_kv = w_kv.reshape(d, 6, NSA_GROUPS, HEAD_DIM).transpose(0, 2, 1, 3).reshape(d, 6 * NSA_KV)
    w_g = w_g.reshape(d, NSA_GROUPS, 3 * NSA_HPG)
    w_g = jnp.pad(w_g, ((0, 0), (0, 0), (0, LANE - 3 * NSA_HPG))).reshape(d, NSA_GROUPS * LANE)
    pair = lambda w: w.reshape(d, 2, DIFF_HEADS, DIFF_DH).transpose(0, 2, 1, 3).reshape(d, 2 * DIFF_QK)
    w_qd = pair(w_qd) * (DIFF_DH ** -0.5)
    w_kd = pair(w_kd)
    return jnp.concatenate([w_qn, w_kv, w_g, w_qd, w_kd, w_vd, w_ga, w_gb], axis=1).astype(BF)


def _sel_weights():
    c0 = jnp.arange(N_CMP)[:, None] * CMP_STRIDE
    s0 = jnp.arange(LANE)[None, :] * SEL_BLOCK
    ov = jnp.minimum(c0 + CMP_BLOCK, s0 + SEL_BLOCK) - jnp.maximum(c0, s0)
    w = jnp.clip(ov, 0, None).astype(F32) / CMP_BLOCK
    return jnp.where(jnp.arange(LANE)[None, :] < N_SEL, w, 0.0).astype(BF)


def _token_mixers(x2, batch, seq, w_in, cmp_pos_k, cmp_pos_v, cmp_k_w1, cmp_k_w2, cmp_v_w1, cmp_v_w2,
                  lam_q1, lam_k1, lam_q2, lam_k2, diff_norm_g):
    qn, kv, gates, qd, kd, vd, ga, gb = _proj(x2, _arrange_w_in(w_in))
    r = kv.reshape(batch, seq, NSA_GROUPS, 6, HEAD_DIM)[:, :, :, 0:2]
    r = r.transpose(0, 2, 3, 1, 4).reshape(batch, NSA_GROUPS, 2, seq // CMP_STRIDE, CMP_STRIDE * HEAD_DIM)
    pos2 = lambda pe: pe.reshape(2, CMP_STRIDE * HEAD_DIM)
    cmp = _compress(r, pos2(cmp_pos_k), pos2(cmp_pos_v), cmp_k_w1.astype(BF), cmp_k_w2.astype(BF),
                    cmp_v_w1.astype(BF), cmp_v_w2.astype(BF))
    o_nsa = _nsa(qn, kv, cmp, gates, _sel_weights(), batch, seq)
    lamv = jnp.stack([lam_q1, lam_k1, lam_q2, lam_k2]).astype(F32)
    o_diff = _diff(qd, kd, vd, lamv, diff_norm_g.reshape(1, DIFF_VDIM).astype(F32), batch, seq)
    return o_nsa, o_diff, ga, gb


def kernel(x, p, w_in, cmp_pos_k, cmp_pos_v, cmp_k_w1, cmp_k_w2, cmp_v_w1, cmp_v_w2, lam_q1, lam_k1, lam_q2, lam_k2, diff_norm_g, w_branch_nsa, w_branch_diff, w_out, ln1_g, ln1_b, peer_wq, peer_subkeys1, peer_subkeys2, peer_u, peer_v, ln2_g, ln2_b, ple_w_proj, ple_w_gate):
    batch, seq, d = x.shape
    assert seq == N_CMP * CMP_STRIDE and seq == N_SEL * SEL_BLOCK and d == D_MODEL
    x2 = x.reshape(batch * seq, d)
    row = lambda v: v.reshape(1, -1).astype(F32)
    o_nsa, o_diff, ga, gb = _token_mixers(
        x2, batch, seq, w_in[0], cmp_pos_k[0], cmp_pos_v[0], cmp_k_w1[0], cmp_k_w2[0], cmp_v_w1[0],
        cmp_v_w2[0], lam_q1[0], lam_k1[0], lam_q2[0], lam_k2[0], diff_norm_g[0])
    h1, h1b = _merge(o_nsa, o_diff, ga, gb, x2, w_branch_nsa[0].astype(BF), w_branch_diff[0].astype(BF),
                     w_out[0].astype(BF), row(ln1_g[0]), row(ln1_b[0]))
    c, p2, e1, e2 = _route(h1b, peer_wq[0].T.astype(BF), peer_subkeys1[0].astype(BF),
                           peer_subkeys2[0].astype(BF))
    ffn_t = _peer(h1b, peer_u[0].astype(BF), peer_v[0].T.astype(BF), c, p2, e1, e2)
    out = _final(ffn_t, h1, p[0].reshape(batch * seq, PLE_DIM), ple_w_gate[0].astype(BF),
                 ple_w_proj[0].astype(BF), row(ln2_g[0]), row(ln2_b[0]))
    return out.reshape(batch, seq, d)
```

```python
import functools
import math

import jax
import jax.numpy as jnp
from jax import lax
from jax.experimental import pallas as pl
from jax.experimental.pallas import tpu as pltpu

D_MODEL = 1024
DEPTH = 1
PLE_DIM = 256
Q_BLOCK = 128

NSA_HEADS = 8
NSA_GROUPS = 2
NSA_HPG = NSA_HEADS // NSA_GROUPS
HEAD_DIM = 64
CMP_BLOCK = 32
CMP_STRIDE = 16
CMP_HIDDEN = 256
SEL_BLOCK = 64
SEL_TOPK = 8
WINDOW = 512
SEL_FORCE = 1.0e4

DIFF_HEADS = 4
DIFF_DH = 64
DIFF_VDIM = 2 * DIFF_DH

PEER_HEADS = 8
PEER_NKEYS = 128
PEER_NEXPERTS = PEER_NKEYS * PEER_NKEYS
PEER_DKEY = 256
PEER_TOPK = 16

NSA_WIDTH = NSA_HEADS * HEAD_DIM
NSA_KV = NSA_GROUPS * HEAD_DIM
DIFF_QK = DIFF_HEADS * DIFF_DH
DIFF_WIDTH = DIFF_HEADS * DIFF_VDIM

DEEPNORM_ALPHA = (2.0 * DEPTH) ** 0.25
NEG = -1.0e30
LN_EPS = 1e-5

BF = jnp.bfloat16
F32 = jnp.float32
I32 = jnp.int32

LANE = 128
BF_TILE = (16, LANE)
KEY_CHUNK = 256
N_CMP = 128
N_SEL = 32
VMEM_LIMIT = 56 * 1024 * 1024

C_QN = 0
C_KV = C_QN + NSA_WIDTH
C_G = C_KV + 6 * NSA_KV
C_QD = C_G + NSA_GROUPS * LANE
C_KD = C_QD + 2 * DIFF_QK
C_VD = C_KD + 2 * DIFF_QK
C_GA = C_VD + DIFF_WIDTH
C_GB = C_GA + D_MODEL
C_END = C_GB + D_MODEL


def _dot(a, b):
    return jnp.dot(a, b, preferred_element_type=F32)


def _dot_nt(a, b):
    return lax.dot_general(a, b, (((1,), (1,)), ((), ())), preferred_element_type=F32)


def _gelu(x):
    return 0.5 * x * (1.0 + lax.erf(x * (2.0 ** -0.5)))


def _layer_norm(y, g, b):
    mu = jnp.mean(y, axis=-1, keepdims=True)
    d = y - mu
    var = jnp.mean(d * d, axis=-1, keepdims=True)
    return d * lax.rsqrt(var + LN_EPS) * g + b


def _proj_kernel(x_ref, w_ref, qn_ref, kv_ref, g_ref, qd_ref, kd_ref, vd_ref, ga_ref, gb_ref):
    x = x_ref[...].astype(BF)

    def seg(c0, c1):
        return _dot(x, w_ref[:, c0:c1])

    qn_ref[...] = seg(C_QN, C_KV).astype(BF)
    kv_ref[...] = seg(C_KV, C_G).astype(BF)
    g_ref[...] = jax.nn.sigmoid(seg(C_G, C_QD))
    qd_ref[...] = seg(C_QD, C_KD).astype(BF)
    kd_ref[...] = seg(C_KD, C_VD).astype(BF)
    vd_ref[...] = seg(C_VD, C_GA).astype(BF)
    ga_ref[...] = jax.nn.sigmoid(seg(C_GA, C_GB)).astype(BF)
    gb_ref[...] = jax.nn.sigmoid(seg(C_GB, C_END)).astype(BF)


def _proj(x2, w_all, tm=512):
    n = x2.shape[0]
    widths = [(C_KV - C_QN, BF), (C_G - C_KV, BF), (C_QD - C_G, F32), (C_KD - C_QD, BF),
              (C_VD - C_KD, BF), (C_GA - C_VD, BF), (C_GB - C_GA, BF), (C_END - C_GB, BF)]
    return pl.pallas_call(
        _proj_kernel,
        grid=(n // tm,),
        in_specs=[pl.BlockSpec((tm, D_MODEL), lambda i: (i, 0)),
                  pl.BlockSpec((D_MODEL, C_END), lambda i: (0, 0))],
        out_specs=[pl.BlockSpec((tm, w), lambda i: (i, 0)) for w, _ in widths],
        out_shape=[jax.ShapeDtypeStruct((n, w), dt) for w, dt in widths],
        compiler_params=pltpu.CompilerParams(dimension_semantics=("arbitrary",),
                                             vmem_limit_bytes=VMEM_LIMIT),
        name="proj",
    )(x2, w_all)


def _compress_kernel(r_ref, posk_ref, posv_ref, wk1_ref, wk2_ref, wv1_ref, wv2_ref, o_ref):
    half = CMP_STRIDE * HEAD_DIM
    for idx, (pos_ref, w1_ref, w2_ref) in enumerate(((posk_ref, wk1_ref, wk2_ref),
                                                     (posv_ref, wv1_ref, wv2_ref))):
        r = r_ref[0, 0, idx].astype(F32)
        ra = (r + pos_ref[0:1, :]).astype(BF)
        rb = (r + pos_ref[1:2, :]).astype(BF)
        p1 = _dot(ra, w1_ref[0:half, :])
        p2 = _dot(rb, w1_ref[half:2 * half, :])
        hid = p1 + pltpu.roll(p2, N_CMP - 1, 0)
        c = _dot(_gelu(hid).astype(BF), w2_ref[...])
        o_ref[0, 0, :, idx * HEAD_DIM:(idx + 1) * HEAD_DIM] = c.astype(BF)


def _compress(r, posk, posv, wk1, wk2, wv1, wv2):
    b = r.shape[0]
    full = lambda a: pl.BlockSpec(a.shape, lambda i, j: (0,) * a.ndim)
    return pl.pallas_call(
        _compress_kernel,
        grid=(b, NSA_GROUPS),
        in_specs=[pl.BlockSpec((1, 1, 2, N_CMP, CMP_STRIDE * HEAD_DIM), lambda i, j: (i, j, 0, 0, 0)),
                  full(posk), full(posv), full(wk1), full(wk2), full(wv1), full(wv2)],
        out_specs=pl.BlockSpec((1, 1, N_CMP, 2 * HEAD_DIM), lambda i, j: (i, j, 0, 0)),
        out_shape=jax.ShapeDtypeStruct((b, NSA_GROUPS, N_CMP, 2 * HEAD_DIM), BF),
        compiler_params=pltpu.CompilerParams(dimension_semantics=("arbitrary", "arbitrary")),
        name="compress",
    )(r, posk, posv, wk1, wk2, wv1, wv2)


def _flash_step(carry, q, k, v, bias, valid):
    m, l, acc = carry
    s = jnp.where(valid, _dot_nt(q, k) - bias, NEG)
    m_new = jnp.maximum(m, jnp.max(s, axis=-1, keepdims=True))
    a = jnp.exp(m - m_new)
    p = jnp.where(valid, jnp.exp(s - m_new), 0.0)
    l = a * l + jnp.sum(p, axis=-1, keepdims=True)
    acc = a * acc + _dot(p.astype(BF), v)
    return m_new, l, acc


def _flash_init(rows, dv):
    return (jnp.full((rows, 1), NEG, F32), jnp.zeros((rows, 1), F32), jnp.zeros((rows, dv), F32))


def _flash_out(carry):
    _, l, acc = carry
    return acc / jnp.maximum(l, 1e-30)


def _nsa_kernel(q_ref, kv_ref, cmp_ref, g_ref, selw_ref, o_ref):
    grp = pl.program_id(1)
    i = pl.program_id(2)
    rows = NSA_HPG * Q_BLOCK
    q = q_ref[...]
    q4 = jnp.concatenate([q[:, h * HEAD_DIM:(h + 1) * HEAD_DIM] for h in range(NSA_HPG)], axis=0)
    row = lax.broadcasted_iota(I32, (rows, 1), 0)
    t_row = i * Q_BLOCK + (row & (Q_BLOCK - 1))
    head = grp * NSA_HPG + (row >> 7)
    slope = jnp.exp2(-(head + 1).astype(F32))

    kc = cmp_ref[0, 0, :, 0:HEAD_DIM]
    vc = cmp_ref[0, 0, :, HEAD_DIM:2 * HEAD_DIM]
    c_end = lax.broadcasted_iota(I32, (1, N_CMP), 1) * CMP_STRIDE + (CMP_BLOCK - 1)
    dist_c = (t_row - c_end).astype(F32)
    valid_c = dist_c >= 0
    s = jnp.where(valid_c, _dot_nt(q4, kc) - slope * dist_c, NEG)
    m = jnp.max(s, axis=-1, keepdims=True)
    e = jnp.where(valid_c, jnp.exp(s - m), 0.0)
    p_cmp = e / jnp.maximum(jnp.sum(e, axis=-1, keepdims=True), 1e-30)
    o_cmp = _dot(p_cmp.astype(BF), vc)

    psum = (p_cmp[0:Q_BLOCK] + p_cmp[Q_BLOCK:2 * Q_BLOCK]
            + p_cmp[2 * Q_BLOCK:3 * Q_BLOCK] + p_cmp[3 * Q_BLOCK:4 * Q_BLOCK])
    p_hi = psum.astype(BF)
    p_lo = (psum - p_hi.astype(F32)).astype(BF)
    selw = selw_ref[...]
    imp = _dot(p_hi, selw) + _dot(p_lo, selw)
    jb = lax.broadcasted_iota(I32, (Q_BLOCK, LANE), 1)
    tq = i * Q_BLOCK + lax.broadcasted_iota(I32, (Q_BLOCK, LANE), 0)
    cur = tq >> 6
    allowed = jb * SEL_BLOCK <= tq
    forced = (jb == 0) | (jb == cur) | (jb == cur - 1)
    score = jnp.where(forced, SEL_FORCE, jnp.where(allowed, imp, -SEL_FORCE))
    score = jnp.where(jb < N_SEL, score, -jnp.inf)
    sel = jnp.zeros((Q_BLOCK, LANE), F32)
    for _ in range(SEL_TOPK):
        mx = jnp.max(score, axis=-1, keepdims=True)
        idx = jnp.min(jnp.where(score == mx, jb, LANE), axis=-1, keepdims=True)
        hit = jb == idx
        sel = jnp.where(hit, 1.0, sel)
        score = jnp.where(hit, -jnp.inf, score)
    sel_b = sel.astype(BF)

    key_iota = lax.broadcasted_iota(I32, (1, KEY_CHUNK), 1)
    blk_iota = lax.broadcasted_iota(I32, (LANE, KEY_CHUNK), 0)
    key_blk = lax.broadcasted_iota(I32, (LANE, KEY_CHUNK), 1) >> 6
    n_chunk = (i >> 1) + 1

    def kv_chunk(j, c0):
        start = pl.multiple_of(j * KEY_CHUNK, KEY_CHUNK)
        return kv_ref[pl.ds(start, KEY_CHUNK), c0:c0 + HEAD_DIM]

    def sel_body(j, carry):
        expand = jnp.where(blk_iota == j * (KEY_CHUNK // SEL_BLOCK) + key_blk, 1.0, 0.0).astype(BF)
        picked = jnp.concatenate([_dot(sel_b, expand)] * NSA_HPG, axis=0) > 0.5
        dist = (t_row - (j * KEY_CHUNK + key_iota)).astype(F32)
        valid = picked & (dist >= 0)
        return _flash_step(carry, q4, kv_chunk(j, 2 * HEAD_DIM), kv_chunk(j, 3 * HEAD_DIM),
                           slope * dist, valid)

    o_slc = _flash_out(lax.fori_loop(0, n_chunk, sel_body, _flash_init(rows, HEAD_DIM)))

    def win_body(j, carry):
        dist = (t_row - (j * KEY_CHUNK + key_iota)).astype(F32)
        valid = (dist >= 0) & (dist < WINDOW)
        return _flash_step(carry, q4, kv_chunk(j, 4 * HEAD_DIM), kv_chunk(j, 5 * HEAD_DIM),
                           slope * dist, valid)

    w_start = jnp.maximum(i - WINDOW // Q_BLOCK, 0) >> 1
    o_win = _flash_out(lax.fori_loop(w_start, n_chunk, win_body, _flash_init(rows, HEAD_DIM)))

    gates = g_ref[...]
    for h in range(NSA_HPG):
        r0 = h * Q_BLOCK
        o_h = (gates[:, 3 * h:3 * h + 1] * o_cmp[r0:r0 + Q_BLOCK]
               + gates[:, 3 * h + 1:3 * h + 2] * o_slc[r0:r0 + Q_BLOCK]
               + gates[:, 3 * h + 2:3 * h + 3] * o_win[r0:r0 + Q_BLOCK])
        o_ref[:, h * HEAD_DIM:(h + 1) * HEAD_DIM] = o_h.astype(BF)


def _nsa(qn, kv, cmp, gates, selw, batch, seq):
    nq = seq // Q_BLOCK
    gw = NSA_HPG * HEAD_DIM
    return pl.pallas_call(
        _nsa_kernel,
        grid=(batch, NSA_GROUPS, nq),
        in_specs=[pl.BlockSpec((Q_BLOCK, gw), lambda b, g, i: (b * nq + i, g)),
                  pl.BlockSpec((seq, 6 * HEAD_DIM), lambda b, g, i: (b, g)),
                  pl.BlockSpec((1, 1, N_CMP, 2 * HEAD_DIM), lambda b, g, i: (b, g, 0, 0)),
                  pl.BlockSpec((Q_BLOCK, LANE), lambda b, g, i: (b * nq + i, g)),
                  pl.BlockSpec((N_CMP, LANE), lambda b, g, i: (0, 0))],
        out_specs=pl.BlockSpec((Q_BLOCK, gw), lambda b, g, i: (b * nq + i, g)),
        out_shape=jax.ShapeDtypeStruct((batch * seq, NSA_WIDTH), BF),
        compiler_params=pltpu.CompilerParams(dimension_semantics=("arbitrary",) * 3),
        name="nsa",
    )(qn, kv, cmp, gates, selw)


def _diff_kernel(q_ref, k_ref, v_ref, lam_ref, ng_ref, o_ref):
    h = pl.program_id(1)
    i = pl.program_id(2)
    q = q_ref[...]
    q1 = q[:, 0:DIFF_DH]
    q2 = q[:, DIFF_DH:2 * DIFF_DH]
    lam_init = 0.8 - 0.6 * math.exp(-0.3 * 0)
    lv = lam_ref[...]
    lam = (jnp.exp(jnp.sum(lv[0:1] * lv[1:2], axis=-1, keepdims=True))
           - jnp.exp(jnp.sum(lv[2:3] * lv[3:4], axis=-1, keepdims=True)) + lam_init)
    slope = jnp.exp2(jnp.full((1, 1), -(8 // DIFF_HEADS) * (h + 1), I32).astype(F32))
    t_row = i * Q_BLOCK + lax.broadcasted_iota(I32, (Q_BLOCK, 1), 0)
    key_iota = lax.broadcasted_iota(I32, (1, KEY_CHUNK), 1)

    def body(j, carry):
        c1, c2 = carry
        start = pl.multiple_of(j * KEY_CHUNK, KEY_CHUNK)
        kk = k_ref[pl.ds(start, KEY_CHUNK), :]
        v = v_ref[pl.ds(start, KEY_CHUNK), :]
        dist = (t_row - (j * KEY_CHUNK + key_iota)).astype(F32)
        valid = dist >= 0
        bias = slope * dist
        c1 = _flash_step(c1, q1, kk[:, 0:DIFF_DH], v, bias, valid)
        c2 = _flash_step(c2, q2, kk[:, DIFF_DH:2 * DIFF_DH], v, bias, valid)
        return c1, c2

    init = (_flash_init(Q_BLOCK, DIFF_VDIM), _flash_init(Q_BLOCK, DIFF_VDIM))
    c1, c2 = lax.fori_loop(0, (i >> 1) + 1, body, init)
    o = _flash_out(c1) - lam * _flash_out(c2)
    o = o * lax.rsqrt(jnp.mean(o * o, axis=-1, keepdims=True) + 1e-5) * ng_ref[...] * (1.0 - lam_init)
    o_ref[...] = o.astype(BF)


def _diff(qd, kd, vd, lamv, ng, batch, seq):
    nq = seq // Q_BLOCK
    return pl.pallas_call(
        _diff_kernel,
        grid=(batch, DIFF_HEADS, nq),
        in_specs=[pl.BlockSpec((Q_BLOCK, 2 * DIFF_DH), lambda b, h, i: (b * nq + i, h)),
                  pl.BlockSpec((seq, 2 * DIFF_DH), lambda b, h, i: (b, h)),
                  pl.BlockSpec((seq, DIFF_VDIM), lambda b, h, i: (b, h)),
                  pl.BlockSpec((4, DIFF_DH), lambda b, h, i: (0, 0)),
                  pl.BlockSpec((1, DIFF_VDIM), lambda b, h, i: (0, 0))],
        out_specs=pl.BlockSpec((Q_BLOCK, DIFF_VDIM), lambda b, h, i: (b * nq + i, h)),
        out_shape=jax.ShapeDtypeStruct((batch * seq, DIFF_WIDTH), BF),
        compiler_params=pltpu.CompilerParams(dimension_semantics=("arbitrary",) * 3),
        name="diff",
    )(qd, kd, vd, lamv, ng)


def _merge_kernel(on_ref, od_ref, ga_ref, gb_ref, x_ref, wn_ref, wd_ref, wo_ref, g_ref, b_ref,
                  h_ref, hb_ref):
    merged = (ga_ref[...].astype(F32) * _dot(on_ref[...], wn_ref[...])
              + gb_ref[...].astype(F32) * _dot(od_ref[...], wd_ref[...]))
    mix = _dot(merged.astype(BF), wo_ref[...])
    h = _layer_norm(DEEPNORM_ALPHA * x_ref[...] + mix, g_ref[...], b_ref[...])
    h_ref[...] = h
    hb_ref[...] = h.astype(BF)


def _merge(o_nsa, o_diff, ga, gb, x2, wn, wd, wo, g, b, tm=256):
    n = x2.shape[0]
    tok = lambda w: pl.BlockSpec((tm, w), lambda i: (i, 0))
    full = lambda a: pl.BlockSpec(a.shape, lambda i: (0, 0))
    return pl.pallas_call(
        _merge_kernel,
        grid=(n // tm,),
        in_specs=[tok(NSA_WIDTH), tok(DIFF_WIDTH), tok(D_MODEL), tok(D_MODEL), tok(D_MODEL),
                  full(wn), full(wd), full(wo), full(g), full(b)],
        out_specs=[tok(D_MODEL), tok(D_MODEL)],
        out_shape=[jax.ShapeDtypeStruct((n, D_MODEL), F32), jax.ShapeDtypeStruct((n, D_MODEL), BF)],
        compiler_params=pltpu.CompilerParams(dimension_semantics=("arbitrary",),
                                             vmem_limit_bytes=VMEM_LIMIT),
        name="merge",
    )(o_nsa, o_diff, ga, gb, x2, wn, wd, wo, g, b)


N_CAND = PEER_TOPK + 8 * (PEER_TOPK - 1)


def _extract_topk(s, k, break_ties):
    nrow, w = s.shape
    rows = lax.broadcasted_iota(I32, (nrow, w), 0)
    krow = lax.broadcasted_iota(I32, (k, w), 0)
    pos = jnp.full((nrow, w), float(k), F32)
    vals = jnp.zeros((k, w), F32)
    for r in range(k):
        m = jnp.max(s, axis=0, keepdims=True)
        if break_ties:
            idx = jnp.min(jnp.where(s == m, rows, nrow), axis=0, keepdims=True)
            hit = rows == idx
        else:
            hit = s == m
        pos = jnp.where(hit, float(r), pos)
        s = jnp.where(hit, -jnp.inf, s)
        vals = jnp.where(krow == r, m, vals)
    exact = jnp.sum(jnp.where(pos < k, 1.0, 0.0), axis=0, keepdims=True) == k
    return vals, pos, exact


def _route_kernel(h_ref, wq_ref, sk1_ref, sk2_ref, c_ref, p2_ref, e1_ref, e2_ref, q_scr):
    t = h_ref.shape[0]
    q_scr[...] = _dot_nt(wq_ref[...], h_ref[...]).astype(BF)
    half = PEER_DKEY // 2

    crow = lax.broadcasted_iota(I32, (N_CAND, LANE), 0)
    ci = jnp.where(crow < PEER_TOPK, 0, ((crow - PEER_TOPK) >> 3) + 1)
    cj = jnp.where(crow < PEER_TOPK, crow, (crow - PEER_TOPK) & 7)
    cand_ok = (ci + 1) * (cj + 1) <= PEER_TOPK
    krow = lax.broadcasted_iota(I32, (PEER_TOPK, LANE), 0)

    def tables(s1, s2, break_ties):
        v1, pos1, ok1 = _extract_topk(s1, PEER_TOPK, break_ties)
        v2, pos2, ok2 = _extract_topk(s2, PEER_TOPK, break_ties)
        pieces = [v1[0:1] + v2]
        for r in range(1, PEER_TOPK):
            pieces.append(v1[r:r + 1] + v2[0:8])
        cand = jnp.where(cand_ok, jnp.concatenate(pieces, axis=0), -jnp.inf)
        _, cpos, ok3 = _extract_topk(cand, PEER_TOPK, break_ties)
        picked = cpos < PEER_TOPK
        top = v1[0:1] + v2[0:1]
        z = jnp.sum(jnp.where(picked, jnp.exp(cand - top), 0.0), axis=0, keepdims=True)
        pk = picked.astype(F32)
        cnt = jnp.zeros((PEER_TOPK, LANE), F32)
        cnt = jnp.where(krow == 0, jnp.sum(pk[0:PEER_TOPK], axis=0, keepdims=True), cnt)
        for r in range(1, PEER_TOPK):
            lo = PEER_TOPK + 8 * (r - 1)
            cnt = jnp.where(krow == r, jnp.sum(pk[lo:lo + 8], axis=0, keepdims=True), cnt)
        c = jnp.zeros((PEER_NKEYS, LANE), F32)
        for r in range(PEER_TOPK):
            c = jnp.where(pos1 == float(r), cnt[r:r + 1], c)
        exact = ok1 & ok2 & ok3
        return (c, pos2, jnp.exp(s1 - v1[0:1]) / z, jnp.exp(s2 - v2[0:1])), exact

    def head_body(hd, _):
        base = pl.multiple_of(hd * PEER_DKEY, PEER_DKEY)
        s1_all = _dot(sk1_ref[...], q_scr[pl.ds(base, half), :])
        s2_all = _dot(sk2_ref[...], q_scr[pl.ds(base + half, half), :])
        for lc in range(t // LANE):
            s1 = s1_all[:, lc * LANE:(lc + 1) * LANE]
            s2 = s2_all[:, lc * LANE:(lc + 1) * LANE]

            def store(vals):
                for ref, v in zip((c_ref, p2_ref, e1_ref, e2_ref), vals):
                    ref[hd, lc] = v

            vals, exact = tables(s1, s2, break_ties=False)
            store(vals)

            @pl.when(jnp.min(jnp.where(exact, 1.0, 0.0)) < 0.5)
            def _():
                store(tables(s1, s2, break_ties=True)[0])
        return 0

    lax.fori_loop(0, PEER_HEADS, head_body, 0)


def _route(hb, wq_t, sk1, sk2, tm=256):
    n = hb.shape[0]
    full = lambda a: pl.BlockSpec(a.shape, lambda i: (0, 0))
    out_spec = pl.BlockSpec((PEER_HEADS, tm // LANE, PEER_NKEYS, LANE), lambda i: (0, i, 0, 0))
    shape = (PEER_HEADS, n // LANE, PEER_NKEYS, LANE)
    return pl.pallas_call(
        _route_kernel,
        grid=(n // tm,),
        in_specs=[pl.BlockSpec((tm, D_MODEL), lambda i: (i, 0)), full(wq_t), full(sk1), full(sk2)],
        out_specs=[out_spec] * 4,
        out_shape=[jax.ShapeDtypeStruct(shape, F32)] * 4,
        scratch_shapes=[pltpu.VMEM((PEER_HEADS * PEER_DKEY, tm), BF)],
        compiler_params=pltpu.CompilerParams(dimension_semantics=("arbitrary",),
                                             vmem_limit_bytes=VMEM_LIMIT),
        name="route",
    )(hb, wq_t, sk1, sk2)


def _peer_kernel(x_ref, u_ref, vt_ref, c_ref, p2_ref, e1_ref, e2_ref, o_ref,
                 act0, act1, y0, y1, p2_scr, e2_scr, *, n_chunks):
    s = pl.program_id(1)
    te, tm = y0.shape
    n_a = te // PEER_NKEYS
    groups = PEER_NKEYS // BF_TILE[0]

    @pl.when(s == 0)
    def _():
        o_ref[...] = jnp.zeros_like(o_ref)
        for r in (act0, act1, y0, y1):
            r[...] = jnp.zeros_like(r)
        for hd in range(PEER_HEADS):
            for lc in range(tm // LANE):
                p2_scr[hd, lc] = p2_ref[hd, lc].astype(BF).reshape(groups, *BF_TILE)
                e2_scr[hd, lc] = e2_ref[hd, lc].astype(BF).reshape(groups, *BF_TILE)

    jb = jnp.clip(s - 1, 0, n_chunks - 1)

    def step(act_new, act_old, y_new, y_old):
        o_ref[...] += _dot(vt_ref[...], y_old[...])
        for al in range(n_a):
            a = jb * n_a + al
            for lc in range(tm // LANE):
                w = None
                for hd in range(PEER_HEADS):
                    ca = jnp.broadcast_to(c_ref[hd, lc, pl.ds(a, 1), :], BF_TILE).astype(BF)[None]
                    e1a = jnp.broadcast_to(e1_ref[hd, lc, pl.ds(a, 1), :], BF_TILE).astype(BF)[None]
                    term = jnp.where(p2_scr[hd, lc] < ca, e2_scr[hd, lc], 0.0) * e1a
                    w = term if w is None else w + term
                rows = slice(al * PEER_NKEYS, (al + 1) * PEER_NKEYS)
                cols = slice(lc * LANE, (lc + 1) * LANE)
                y = w * act_old[rows, cols].reshape(groups, *BF_TILE)
                y_new[rows, cols] = y.reshape(PEER_NKEYS, LANE)
        act_new[...] = _gelu(_dot_nt(u_ref[...], x_ref[...])).astype(BF)

    @pl.when(s % 2 == 0)
    def _():
        step(act0, act1, y1, y0)

    @pl.when(s % 2 == 1)
    def _():
        step(act1, act0, y0, y1)


def _peer(hb, u_b, vt_b, c, p2, e1, e2, tm=512, te=1024):
    n = hb.shape[0]
    n_chunks = u_b.shape[0] // te
    last = n_chunks - 1
    rt = pl.BlockSpec((PEER_HEADS, tm // LANE, PEER_NKEYS, LANE), lambda i, s: (0, i, 0, 0))
    packed = pltpu.VMEM((PEER_HEADS, tm // LANE, PEER_NKEYS // BF_TILE[0]) + BF_TILE, BF)
    chunk = pltpu.VMEM((te, tm), BF)
    return pl.pallas_call(
        functools.partial(_peer_kernel, n_chunks=n_chunks),
        grid=(n // tm, n_chunks + 2),
        in_specs=[pl.BlockSpec((tm, D_MODEL), lambda i, s: (i, 0)),
                  pl.BlockSpec((te, D_MODEL), lambda i, s: (jnp.minimum(s, last), 0)),
                  pl.BlockSpec((D_MODEL, te), lambda i, s: (0, jnp.clip(s - 2, 0, last))),
                  rt, rt, rt, rt],
        out_specs=pl.BlockSpec((D_MODEL, tm), lambda i, s: (0, i)),
        out_shape=jax.ShapeDtypeStruct((D_MODEL, n), F32),
        scratch_shapes=[chunk, chunk, chunk, chunk, packed, packed],
        compiler_params=pltpu.CompilerParams(dimension_semantics=("arbitrary", "arbitrary"),
                                             vmem_limit_bytes=VMEM_LIMIT),
        name="peer",
    )(hb, u_b, vt_b, c, p2, e1, e2)


def _final_kernel(ft_ref, h_ref, p_ref, wg_ref, wp_ref, g_ref, b_ref, o_ref):
    ffn = ft_ref[...].T
    h = _layer_norm(DEEPNORM_ALPHA * h_ref[...] + ffn, g_ref[...], b_ref[...])
    gate = jax.nn.sigmoid(_dot(h.astype(BF), wg_ref[...]))
    o_ref[...] = h + gate * _dot(p_ref[...].astype(BF), wp_ref[...])


def _final(ffn_t, h1, p2d, wg, wp, g, b, tm=256):
    n = h1.shape[0]
    full = lambda a: pl.BlockSpec(a.shape, lambda i: (0, 0))
    return pl.pallas_call(
        _final_kernel,
        grid=(n // tm,),
        in_specs=[pl.BlockSpec((D_MODEL, tm), lambda i: (0, i)),
                  pl.BlockSpec((tm, D_MODEL), lambda i: (i, 0)),
                  pl.BlockSpec((tm, PLE_DIM), lambda i: (i, 0)),
                  full(wg), full(wp), full(g), full(b)],
        out_specs=pl.BlockSpec((tm, D_MODEL), lambda i: (i, 0)),
        out_shape=jax.ShapeDtypeStruct((n, D_MODEL), F32),
        compiler_params=pltpu.CompilerParams(dimension_semantics=("arbitrary",),
                                             vmem_limit_bytes=VMEM_LIMIT),
        name="final",
    )(ffn_t, h1, p2d, wg, wp, g, b)


def _arrange_w_in(w_in):
    cuts = [NSA_WIDTH, 6 * NSA_KV, 3 * NSA_HEADS, 2 * DIFF_QK, 2 * DIFF_QK, DIFF_WIDTH, D_MODEL, D_MODEL]
    offs = [0]
    for c in cuts:
        offs.append(offs[-1] + c)
    w_qn, w_kv, w_g, w_qd, w_kd, w_vd, w_ga, w_gb = [w_in[:, offs[k]:offs[k + 1]] for k in range(8)]
    d = w_in.shape[0]
    w_qn = w_qn * (HEAD_DIM ** -0.5)
    w_kv = w_kv.reshape(d, 6, NSA_GROUPS, HEAD_DIM).transpose(0, 2, 1, 3).reshape(d, 6 * NSA_KV)
    w_g = w_g.reshape(d, NSA_GROUPS, 3 * NSA_HPG)
    w_g = jnp.pad(w_g, ((0, 0), (0, 0), (0, LANE - 3 * NSA_HPG))).reshape(d, NSA_GROUPS * LANE)
    pair = lambda w: w.reshape(d, 2, DIFF_HEADS, DIFF_DH).transpose(0, 2, 1, 3).reshape(d, 2 * DIFF_QK)
    w_qd = pair(w_qd) * (DIFF_DH ** -0.5)
    w_kd = pair(w_kd)
    return jnp.concatenate([w_qn, w_kv, w_g, w_qd, w_kd, w_vd, w_ga, w_gb], axis=1).astype(BF)


def _sel_weights():
    c0 = jnp.arange(N_CMP)[:, None] * CMP_STRIDE
    s0 = jnp.arange(LANE)[None, :] * SEL_BLOCK
    ov = jnp.minimum(c0 + CMP_BLOCK, s0 + SEL_BLOCK) - jnp.maximum(c0, s0)
    w = jnp.clip(ov, 0, None).astype(F32) / CMP_BLOCK
    return jnp.where(jnp.arange(LANE)[None, :] < N_SEL, w, 0.0).astype(BF)


def _token_mixers(x2, batch, seq, w_in, cmp_pos_k, cmp_pos_v, cmp_k_w1, cmp_k_w2, cmp_v_w1, cmp_v_w2,
                  lam_q1, lam_k1, lam_q2, lam_k2, diff_norm_g):
    qn, kv, gates, qd, kd, vd, ga, gb = _proj(x2, _arrange_w_in(w_in))
    r = kv.reshape(batch, seq, NSA_GROUPS, 6, HEAD_DIM)[:, :, :, 0:2]
    r = r.transpose(0, 2, 3, 1, 4).reshape(batch, NSA_GROUPS, 2, seq // CMP_STRIDE, CMP_STRIDE * HEAD_DIM)
    pos2 = lambda pe: pe.reshape(2, CMP_STRIDE * HEAD_DIM)
    cmp = _compress(r, pos2(cmp_pos_k), pos2(cmp_pos_v), cmp_k_w1.astype(BF), cmp_k_w2.astype(BF),
                    cmp_v_w1.astype(BF), cmp_v_w2.astype(BF))
    o_nsa = _nsa(qn, kv, cmp, gates, _sel_weights(), batch, seq)
    lamv = jnp.stack([lam_q1, lam_k1, lam_q2, lam_k2]).astype(F32)
    o_diff = _diff(qd, kd, vd, lamv, diff_norm_g.reshape(1, DIFF_VDIM).astype(F32), batch, seq)
    return o_nsa, o_diff, ga, gb


def kernel(x, p, w_in, cmp_pos_k, cmp_pos_v, cmp_k_w1, cmp_k_w2, cmp_v_w1, cmp_v_w2, lam_q1, lam_k1, lam_q2, lam_k2, diff_norm_g, w_branch_nsa, w_branch_diff, w_out, ln1_g, ln1_b, peer_wq, peer_subkeys1, peer_subkeys2, peer_u, peer_v, ln2_g, ln2_b, ple_w_proj, ple_w_gate):
    batch, seq, d = x.shape
    assert seq == N_CMP * CMP_STRIDE and seq == N_SEL * SEL_BLOCK and d == D_MODEL
    x2 = x.reshape(batch * seq, d)
    row = lambda v: v.reshape(1, -1).astype(F32)
    o_nsa, o_diff, ga, gb = _token_mixers(
        x2, batch, seq, w_in[0], cmp_pos_k[0], cmp_pos_v[0], cmp_k_w1[0], cmp_k_w2[0], cmp_v_w1[0],
        cmp_v_w2[0], lam_q1[0], lam_k1[0], lam_q2[0], lam_k2[0], diff_norm_g[0])
    h1, h1b = _merge(o_nsa, o_diff, ga, gb, x2, w_branch_nsa[0].astype(BF), w_branch_diff[0].astype(BF),
                     w_out[0].astype(BF), row(ln1_g[0]), row(ln1_b[0]))
    c, p2, e1, e2 = _route(h1b, peer_wq[0].T.astype(BF), peer_subkeys1[0].astype(BF),
                           peer_subkeys2[0].astype(BF))
    ffn_t = _peer(h1b, peer_u[0].astype(BF), peer_v[0].T.astype(BF), c, p2, e1, e2)
    out = _final(ffn_t, h1, p[0].reshape(batch * seq, PLE_DIM), ple_w_gate[0].astype(BF),
                 ple_w_proj[0].astype(BF), row(ln2_g[0]), row(ln2_b[0]))
    return out.reshape(batch, seq, d)
```

```python
import functools
import math

import jax
import jax.numpy as jnp
from jax import lax
from jax.experimental import pallas as pl
from jax.experimental.pallas import tpu as pltpu

D_MODEL = 1024
DEPTH = 1
PLE_DIM = 256

NSA_HEADS = 8
NSA_GROUPS = 2
NSA_HPG = NSA_HEADS // NSA_GROUPS
HEAD_DIM = 64
CMP_BLOCK = 32
CMP_STRIDE = 16
SEL_BLOCK = 64
SEL_TOPK = 8
WINDOW = 512
SEL_FORCE = 1.0e4

DIFF_HEADS = 4
DIFF_DH = 64
DIFF_VDIM = 2 * DIFF_DH

PEER_HEADS = 8
PEER_NKEYS = 128
PEER_NEXPERTS = PEER_NKEYS * PEER_NKEYS
PEER_DKEY = 256
PEER_TOPK = 16

NSA_WIDTH = NSA_HEADS * HEAD_DIM
NSA_KV = NSA_GROUPS * HEAD_DIM
DIFF_QK = DIFF_HEADS * DIFF_DH
DIFF_WIDTH = DIFF_HEADS * DIFF_VDIM

DEEPNORM_ALPHA = (2.0 * DEPTH) ** 0.25
LN_EPS = 1e-5

BF = jnp.bfloat16
F32 = jnp.float32
I32 = jnp.int32

LANE = 128
BF_TILE = (16, LANE)
Q_TILE_NSA = 256
Q_TILE_DIFF = 512
N_CMP = 128
N_SEL = 32
PEER_CHUNK = 512
VMEM_LIMIT = 56 * 1024 * 1024

AUG = HEAD_DIM
SEL_LANE0 = AUG + 8
M_INIT = -1.0e30
MASKED = -(2.0 ** 100)
GATE_ROWS = 16

PROJ_COLS = (("ksel", NSA_GROUPS * LANE, BF), ("kwin", NSA_GROUPS * LANE, BF), ("vsw", NSA_GROUPS * LANE, BF),
             ("kvc", NSA_GROUPS * LANE, BF), ("kd", 2 * DIFF_HEADS * LANE, BF), ("vd", DIFF_WIDTH, BF),
             ("ga", D_MODEL, BF), ("gb", D_MODEL, BF))
PROJ_WIDTH = sum(w for _, w, _ in PROJ_COLS)
PROJ_ROWS = (("qn", NSA_HEADS * LANE, BF), ("g", NSA_GROUPS * GATE_ROWS, F32), ("qd", 2 * DIFF_HEADS * LANE, BF))
PROJ_HEIGHT = sum(r for _, r, _ in PROJ_ROWS)


def _dot(a, b):
    return jnp.dot(a, b, preferred_element_type=F32)


def _dot_nt(a, b):
    return lax.dot_general(a, b, (((1,), (1,)), ((), ())), preferred_element_type=F32)


def _dot_tn(a, b):
    return lax.dot_general(a, b, (((0,), (0,)), ((), ())), preferred_element_type=F32)


def _gelu(x):
    return 0.5 * x * (1.0 + lax.erf(x * (2.0 ** -0.5)))


def _layer_norm(y, g, b):
    mu = jnp.mean(y, axis=-1, keepdims=True)
    d = y - mu
    var = jnp.mean(d * d, axis=-1, keepdims=True)
    return d * lax.rsqrt(var + LN_EPS) * g + b


def _split_bf16(v):
    hi = v.astype(BF).astype(F32)
    return hi, v - hi


def _proj_kernel(x_ref, w_ref, wt_ref, qaug_ref, *o_refs, seq):
    i = pl.program_id(0)
    tm = x_ref.shape[0]
    x = x_ref[...].astype(BF)
    t = (i * tm + lax.broadcasted_iota(I32, (tm, LANE), 0)) & (seq - 1)
    lane = lax.broadcasted_iota(I32, (tm, LANE), 1)
    hi, lo = _split_bf16(t.astype(F32))
    k_aug = jnp.where(lane == AUG, hi, jnp.where(lane == AUG + 1, lo, 0.0))
    k_aug_sel = k_aug + jnp.where(lane - SEL_LANE0 == (t >> 6), 1.0, 0.0)
    extra = {"ksel": lambda: jnp.concatenate([k_aug_sel] * NSA_GROUPS, axis=1),
             "kwin": lambda: jnp.concatenate([k_aug] * NSA_GROUPS, axis=1),
             "kd": lambda: jnp.concatenate([k_aug] * (2 * DIFF_HEADS), axis=1)}
    c0 = 0
    for (name, width, dt), o_ref in zip(PROJ_COLS, o_refs):
        v = _dot(x, w_ref[:, c0:c0 + width])
        c0 += width
        if name in extra:
            v = v + extra[name]()
        if name in ("ga", "gb"):
            v = jax.nn.sigmoid(v)
        o_ref[...] = v.astype(dt)
    r0 = 0
    for (name, rows, dt), o_ref in zip(PROJ_ROWS, o_refs[len(PROJ_COLS):]):
        v = _dot_nt(wt_ref[r0:r0 + rows, :], x) + qaug_ref[r0:r0 + rows, :]
        r0 += rows
        if name == "g":
            v = jax.nn.sigmoid(v)
        o_ref[...] = v.astype(dt)


def _proj(x2, w_all, wt_all, qaug, seq, tm=256):
    n = x2.shape[0]
    assert seq & (seq - 1) == 0 and seq % tm == 0
    full = lambda a: pl.BlockSpec(a.shape, lambda i: (0, 0))
    return pl.pallas_call(
        functools.partial(_proj_kernel, seq=seq),
        grid=(n // tm,),
        in_specs=[pl.BlockSpec((tm, D_MODEL), lambda i: (i, 0)), full(w_all), full(wt_all), full(qaug)],
        out_specs=([pl.BlockSpec((tm, w), lambda i: (i, 0)) for _, w, _ in PROJ_COLS]
                   + [pl.BlockSpec((r, tm), lambda i: (0, i)) for _, r, _ in PROJ_ROWS]),
        out_shape=([jax.ShapeDtypeStruct((n, w), dt) for _, w, dt in PROJ_COLS]
                   + [jax.ShapeDtypeStruct((r, n), dt) for _, r, dt in PROJ_ROWS]),
        compiler_params=pltpu.CompilerParams(dimension_semantics=("arbitrary",),
                                             vmem_limit_bytes=VMEM_LIMIT),
        name="proj",
    )(x2, w_all, wt_all, qaug)


def _compress_kernel(r_ref, posk_ref, posv_ref, wk1_ref, wk2_ref, wv1_ref, wv2_ref, o_ref):
    half = CMP_STRIDE * HEAD_DIM
    for idx, (pos_ref, w1_ref, w2_ref) in enumerate(((posk_ref, wk1_ref, wk2_ref),
                                                     (posv_ref, wv1_ref, wv2_ref))):
        r = r_ref[0, 0, idx].astype(F32)
        ra = (r + pos_ref[0:1, :]).astype(BF)
        rb = (r + pos_ref[1:2, :]).astype(BF)
        p1 = _dot(ra, w1_ref[0:half, :])
        p2 = _dot(rb, w1_ref[half:2 * half, :])
        hid = p1 + pltpu.roll(p2, N_CMP - 1, 0)
        c = _dot(_gelu(hid).astype(BF), w2_ref[...])
        o_ref[0, 0, :, idx * LANE:idx * LANE + HEAD_DIM] = c.astype(BF)
    lane = lax.broadcasted_iota(I32, (N_CMP, HEAD_DIM), 1)
    c_end = lax.broadcasted_iota(I32, (N_CMP, HEAD_DIM), 0) * CMP_STRIDE + (CMP_BLOCK - 1)
    hi, lo = _split_bf16(c_end.astype(F32))
    o_ref[0, 0, :, AUG:LANE] = jnp.where(lane == 0, hi, jnp.where(lane == 1, lo, 0.0)).astype(BF)
    o_ref[0, 0, :, LANE + HEAD_DIM:2 * LANE] = jnp.zeros((N_CMP, HEAD_DIM), BF)


def _compress(r, posk, posv, wk1, wk2, wv1, wv2):
    b = r.shape[0]
    full = lambda a: pl.BlockSpec(a.shape, lambda i, j: (0,) * a.ndim)
    return pl.pallas_call(
        _compress_kernel,
        grid=(b, NSA_GROUPS),
        in_specs=[pl.BlockSpec((1, 1, 2, N_CMP, CMP_STRIDE * HEAD_DIM), lambda i, j: (i, j, 0, 0, 0)),
                  full(posk), full(posv), full(wk1), full(wk2), full(wv1), full(wv2)],
        out_specs=pl.BlockSpec((1, 1, N_CMP, 2 * LANE), lambda i, j: (i, j, 0, 0)),
        out_shape=jax.ShapeDtypeStruct((b, NSA_GROUPS, N_CMP, 2 * LANE), BF),
        compiler_params=pltpu.CompilerParams(dimension_semantics=("arbitrary", "arbitrary")),
        name="compress",
    )(r, posk, posv, wk1, wk2, wv1, wv2)


def _flash_step(carry, s, v):
    m, l, acc = carry
    m_new = jnp.maximum(m, jnp.max(s, axis=0, keepdims=True))
    a = jnp.exp(m - m_new)
    p = jnp.exp(s - m_new)
    l = a * l + jnp.sum(p, axis=0, keepdims=True)
    acc = a * acc + _dot_tn(v, p.astype(BF))
    return m_new, l, acc


def _flash_init(rows, dv):
    return (jnp.full((1, rows), M_INIT, F32), jnp.zeros((1, rows), F32), jnp.zeros((dv, rows), F32))


def _flash_out(carry):
    _, l, acc = carry
    return acc / jnp.maximum(l, 1e-30)


def _nsa_kernel(qt_ref, ksel_ref, kwin_ref, vsw_ref, cmp_ref, gt_ref, selwt_ref, o_ref):
    i = pl.program_id(2)
    tq = ck = Q_TILE_NSA
    rows = NSA_HPG * tq
    qt = qt_ref[...]
    q4 = jnp.concatenate([qt[h * LANE:(h + 1) * LANE, :] for h in range(NSA_HPG)], axis=1)
    t_lane = i * tq + (lax.broadcasted_iota(I32, (1, rows), 1) & (tq - 1))
    key_col = lax.broadcasted_iota(I32, (ck, 1), 0)
    causal = (i * ck + key_col) <= t_lane

    kc = cmp_ref[0, 0, :, 0:LANE]
    vc = cmp_ref[0, 0, :, LANE:LANE + HEAD_DIM]
    c_end = lax.broadcasted_iota(I32, (N_CMP, 1), 0) * CMP_STRIDE + (CMP_BLOCK - 1)
    s = jnp.where(c_end <= t_lane, _dot(kc, q4), MASKED)
    m = jnp.maximum(jnp.max(s, axis=0, keepdims=True), M_INIT)
    e = jnp.exp(s - m)
    p_cmp = e / jnp.maximum(jnp.sum(e, axis=0, keepdims=True), 1e-30)
    o_cmp = _dot_tn(vc, p_cmp.astype(BF))

    psum = p_cmp[:, 0:tq] + p_cmp[:, tq:2 * tq] + p_cmp[:, 2 * tq:3 * tq] + p_cmp[:, 3 * tq:4 * tq]
    p_hi = psum.astype(BF)
    p_lo = (psum - p_hi.astype(F32)).astype(BF)
    selwt = selwt_ref[...]
    imp = (_dot(selwt, p_hi) + _dot(selwt, p_lo))[0:N_SEL]
    jb = lax.broadcasted_iota(I32, (N_SEL, tq), 0)
    tq_pos = i * tq + lax.broadcasted_iota(I32, (N_SEL, tq), 1)
    cur = tq_pos >> 6
    forced = (jb == 0) | (jb == cur) | (jb == cur - 1)
    score = jnp.where(forced, SEL_FORCE, jnp.where(jb * SEL_BLOCK <= tq_pos, imp, -SEL_FORCE))
    unsel = jnp.full((N_SEL, tq), MASKED, F32)
    for _ in range(SEL_TOPK):
        mx = jnp.max(score, axis=0, keepdims=True)
        idx = jnp.min(jnp.where(score == mx, jb, N_SEL), axis=0, keepdims=True)
        hit = jb == idx
        unsel = jnp.where(hit, 0.0, unsel)
        score = jnp.where(hit, -jnp.inf, score)
    neg = jnp.concatenate([jnp.zeros((SEL_LANE0, tq), F32), unsel,
                           jnp.zeros((LANE - SEL_LANE0 - N_SEL, tq), F32)], axis=0)
    q4s = (q4.astype(F32) + jnp.concatenate([neg] * NSA_HPG, axis=1)).astype(BF)

    def chunk(ref, j, c0, width):
        start = pl.multiple_of(j * ck, ck)
        return ref[pl.ds(start, ck), c0:c0 + width]

    def sel_body(j, carry):
        return _flash_step(carry, _dot(chunk(ksel_ref, j, 0, LANE), q4s), chunk(vsw_ref, j, 0, HEAD_DIM))

    carry = lax.fori_loop(0, i, sel_body, _flash_init(rows, HEAD_DIM))
    s = jnp.where(causal, _dot(chunk(ksel_ref, i, 0, LANE), q4s), MASKED)
    o_slc = _flash_out(_flash_step(carry, s, chunk(vsw_ref, i, 0, HEAD_DIM)))

    carry = _flash_init(rows, HEAD_DIM)
    j2 = jnp.maximum(i - 2, 0)
    in_window = (t_lane - (j2 * ck + key_col) < WINDOW) & (i >= 2)
    s = jnp.where(in_window, _dot(chunk(kwin_ref, j2, 0, LANE), q4), MASKED)
    carry = _flash_step(carry, s, chunk(vsw_ref, j2, HEAD_DIM, HEAD_DIM))
    j1 = jnp.maximum(i - 1, 0)
    s = jnp.where(i >= 1, _dot(chunk(kwin_ref, j1, 0, LANE), q4), MASKED)
    carry = _flash_step(carry, s, chunk(vsw_ref, j1, HEAD_DIM, HEAD_DIM))
    s = jnp.where(causal, _dot(chunk(kwin_ref, i, 0, LANE), q4), MASKED)
    o_win = _flash_out(_flash_step(carry, s, chunk(vsw_ref, i, HEAD_DIM, HEAD_DIM)))

    gates = gt_ref[...]
    outs = []
    for h in range(NSA_HPG):
        c0 = h * tq
        outs.append(gates[3 * h:3 * h + 1] * o_cmp[:, c0:c0 + tq]
                    + gates[3 * h + 1:3 * h + 2] * o_slc[:, c0:c0 + tq]
                    + gates[3 * h + 2:3 * h + 3] * o_win[:, c0:c0 + tq])
    o_ref[...] = jnp.concatenate(outs, axis=0).T.astype(BF)


def _nsa(qn_t, ksel, kwin, vsw, cmp, gates_t, selwt, batch, seq):
    tq = Q_TILE_NSA
    assert WINDOW == 2 * tq and seq % tq == 0
    nq = seq // tq
    per_seq = pl.BlockSpec((seq, LANE), lambda b, g, i: (b, g))
    return pl.pallas_call(
        _nsa_kernel,
        grid=(batch, NSA_GROUPS, nq),
        in_specs=[pl.BlockSpec((NSA_HPG * LANE, tq), lambda b, g, i: (g, b * nq + i)),
                  per_seq, per_seq, per_seq,
                  pl.BlockSpec((1, 1, N_CMP, 2 * LANE), lambda b, g, i: (b, g, 0, 0)),
                  pl.BlockSpec((GATE_ROWS, tq), lambda b, g, i: (g, b * nq + i)),
                  pl.BlockSpec((LANE, LANE), lambda b, g, i: (0, 0))],
        out_specs=pl.BlockSpec((tq, NSA_HPG * HEAD_DIM), lambda b, g, i: (b * nq + i, g)),
        out_shape=jax.ShapeDtypeStruct((batch * seq, NSA_WIDTH), BF),
        compiler_params=pltpu.CompilerParams(dimension_semantics=("arbitrary",) * 3,
                                             vmem_limit_bytes=VMEM_LIMIT),
        name="nsa",
    )(qn_t, ksel, kwin, vsw, cmp, gates_t, selwt)


def _diff_kernel(qt_ref, k_ref, v_ref, lam_ref, ng_ref, o_ref):
    i = pl.program_id(2)
    tq = ck = Q_TILE_DIFF
    q1 = qt_ref[0:LANE, :]
    q2 = qt_ref[LANE:2 * LANE, :]
    lam_init = 0.8 - 0.6 * math.exp(-0.3 * 0)
    lv = lam_ref[...]
    lam = (jnp.exp(jnp.sum(lv[0:1] * lv[1:2], axis=-1, keepdims=True))
           - jnp.exp(jnp.sum(lv[2:3] * lv[3:4], axis=-1, keepdims=True)) + lam_init)
    t_lane = i * tq + lax.broadcasted_iota(I32, (1, tq), 1)
    causal = (i * ck + lax.broadcasted_iota(I32, (ck, 1), 0)) <= t_lane

    def chunk(j):
        start = pl.multiple_of(j * ck, ck)
        return (k_ref[pl.ds(start, ck), 0:LANE], k_ref[pl.ds(start, ck), LANE:2 * LANE],
                v_ref[pl.ds(start, ck), :])

    def body(j, carry):
        c1, c2 = carry
        k1, k2, v = chunk(j)
        return _flash_step(c1, _dot(k1, q1), v), _flash_step(c2, _dot(k2, q2), v)

    init = (_flash_init(tq, DIFF_VDIM), _flash_init(tq, DIFF_VDIM))
    c1, c2 = lax.fori_loop(0, i, body, init)
    k1, k2, v = chunk(i)
    c1 = _flash_step(c1, jnp.where(causal, _dot(k1, q1), MASKED), v)
    c2 = _flash_step(c2, jnp.where(causal, _dot(k2, q2), MASKED), v)
    o = _flash_out(c1) - lam * _flash_out(c2)
    o = o * lax.rsqrt(jnp.mean(o * o, axis=0, keepdims=True) + 1e-5) * ng_ref[...] * (1.0 - lam_init)
    o_ref[...] = o.T.astype(BF)


def _diff(qd_t, kd, vd, lamv, ng_col, batch, seq):
    tq = Q_TILE_DIFF
    nq = seq // tq
    return pl.pallas_call(
        _diff_kernel,
        grid=(batch, DIFF_HEADS, nq),
        in_specs=[pl.BlockSpec((2 * LANE, tq), lambda b, h, i: (h, b * nq + i)),
                  pl.BlockSpec((seq, 2 * LANE), lambda b, h, i: (b, h)),
                  pl.BlockSpec((seq, DIFF_VDIM), lambda b, h, i: (b, h)),
                  pl.BlockSpec((4, DIFF_DH), lambda b, h, i: (0, 0)),
                  pl.BlockSpec((DIFF_VDIM, 1), lambda b, h, i: (0, 0))],
        out_specs=pl.BlockSpec((tq, DIFF_VDIM), lambda b, h, i: (b * nq + i, h)),
        out_shape=jax.ShapeDtypeStruct((batch * seq, DIFF_WIDTH), BF),
        compiler_params=pltpu.CompilerParams(dimension_semantics=("arbitrary",) * 3,
                                             vmem_limit_bytes=VMEM_LIMIT),
        name="diff",
    )(qd_t, kd, vd, lamv, ng_col)


def _merge_kernel(on_ref, od_ref, ga_ref, gb_ref, x_ref, wn_ref, wd_ref, wo_ref, g_ref, b_ref,
                  h_ref, hb_ref):
    merged = (ga_ref[...].astype(F32) * _dot(on_ref[...], wn_ref[...])
              + gb_ref[...].astype(F32) * _dot(od_ref[...], wd_ref[...]))
    mix = _dot(merged.astype(BF), wo_ref[...])
    h = _layer_norm(DEEPNORM_ALPHA * x_ref[...] + mix, g_ref[...], b_ref[...])
    h_ref[...] = h
    hb_ref[...] = h.T.astype(BF)


def _merge(o_nsa, o_diff, ga, gb, x2, wn, wd, wo, g, b, tm=256):
    n = x2.shape[0]
    tok = lambda w: pl.BlockSpec((tm, w), lambda i: (i, 0))
    full = lambda a: pl.BlockSpec(a.shape, lambda i: (0, 0))
    return pl.pallas_call(
        _merge_kernel,
        grid=(n // tm,),
        in_specs=[tok(NSA_WIDTH), tok(DIFF_WIDTH), tok(D_MODEL), tok(D_MODEL), tok(D_MODEL),
                  full(wn), full(wd), full(wo), full(g), full(b)],
        out_specs=[tok(D_MODEL), pl.BlockSpec((D_MODEL, tm), lambda i: (0, i))],
        out_shape=[jax.ShapeDtypeStruct((n, D_MODEL), F32), jax.ShapeDtypeStruct((D_MODEL, n), BF)],
        compiler_params=pltpu.CompilerParams(dimension_semantics=("arbitrary",),
                                             vmem_limit_bytes=VMEM_LIMIT),
        name="merge",
    )(o_nsa, o_diff, ga, gb, x2, wn, wd, wo, g, b)


N_CAND = PEER_TOPK + 8 * (PEER_TOPK - 1)


def _extract_topk(s, k, break_ties):
    nrow, w = s.shape
    rows = lax.broadcasted_iota(I32, (nrow, w), 0)
    krow = lax.broadcasted_iota(I32, (k, w), 0)
    pos = jnp.full((nrow, w), float(k), F32)
    vals = jnp.zeros((k, w), F32)
    for r in range(k):
        m = jnp.max(s, axis=0, keepdims=True)
        if break_ties:
            idx = jnp.min(jnp.where(s == m, rows, nrow), axis=0, keepdims=True)
            hit = rows == idx
        else:
            hit = s == m
        pos = jnp.where(hit, float(r), pos)
        s = jnp.where(hit, -jnp.inf, s)
        vals = jnp.where(krow == r, m, vals)
    exact = jnp.sum(jnp.where(pos < k, 1.0, 0.0), axis=0, keepdims=True) == k
    return vals, pos, exact


def _route_kernel(h_ref, wq_ref, sk1_ref, sk2_ref, c_ref, p2_ref, e1_ref, e2_ref, q_scr):
    t = h_ref.shape[1]
    q_scr[...] = _dot(wq_ref[...], h_ref[...]).astype(BF)
    half = PEER_DKEY // 2

    crow = lax.broadcasted_iota(I32, (N_CAND, LANE), 0)
    ci = jnp.where(crow < PEER_TOPK, 0, ((crow - PEER_TOPK) >> 3) + 1)
    cj = jnp.where(crow < PEER_TOPK, crow, (crow - PEER_TOPK) & 7)
    cand_ok = (ci + 1) * (cj + 1) <= PEER_TOPK
    krow = lax.broadcasted_iota(I32, (PEER_TOPK, LANE), 0)

    def tables(s1, s2, break_ties):
        v1, pos1, ok1 = _extract_topk(s1, PEER_TOPK, break_ties)
        v2, pos2, ok2 = _extract_topk(s2, PEER_TOPK, break_ties)
        pieces = [v1[0:1] + v2]
        for r in range(1, PEER_TOPK):
            pieces.append(v1[r:r + 1] + v2[0:8])
        cand = jnp.where(cand_ok, jnp.concatenate(pieces, axis=0), -jnp.inf)
        _, cpos, ok3 = _extract_topk(cand, PEER_TOPK, break_ties)
        picked = cpos < PEER_TOPK
        top = v1[0:1] + v2[0:1]
        z = jnp.sum(jnp.where(picked, jnp.exp(cand - top), 0.0), axis=0, keepdims=True)
        pk = picked.astype(F32)
        cnt = jnp.zeros((PEER_TOPK, LANE), F32)
        cnt = jnp.where(krow == 0, jnp.sum(pk[0:PEER_TOPK], axis=0, keepdims=True), cnt)
        for r in range(1, PEER_TOPK):
            lo = PEER_TOPK + 8 * (r - 1)
            cnt = jnp.where(krow == r, jnp.sum(pk[lo:lo + 8], axis=0, keepdims=True), cnt)
        c = jnp.zeros((PEER_NKEYS, LANE), F32)
        for r in range(PEER_TOPK):
            c = jnp.where(pos1 == float(r), cnt[r:r + 1], c)
        exact = ok1 & ok2 & ok3
        return (c, pos2, jnp.exp(s1 - v1[0:1]) / z, jnp.exp(s2 - v2[0:1])), exact

    def head_body(hd, _):
        base = pl.multiple_of(hd * PEER_DKEY, PEER_DKEY)
        s1_all = _dot(sk1_ref[...], q_scr[pl.ds(base, half), :])
        s2_all = _dot(sk2_ref[...], q_scr[pl.ds(base + half, half), :])
        for lc in range(t // LANE):
            s1 = s1_all[:, lc * LANE:(lc + 1) * LANE]
            s2 = s2_all[:, lc * LANE:(lc + 1) * LANE]

            def store(vals):
                for ref, v in zip((c_ref, p2_ref, e1_ref, e2_ref), vals):
                    ref[hd, lc] = v

            vals, exact = tables(s1, s2, break_ties=False)
            store(vals)

            @pl.when(jnp.min(jnp.where(exact, 1.0, 0.0)) < 0.5)
            def _():
                store(tables(s1, s2, break_ties=True)[0])
        return 0

    lax.fori_loop(0, PEER_HEADS, head_body, 0)


def _route(hb_t, wq_t, sk1, sk2, tm=256):
    n = hb_t.shape[1]
    full = lambda a: pl.BlockSpec(a.shape, lambda i: (0, 0))
    out_spec = pl.BlockSpec((PEER_HEADS, tm // LANE, PEER_NKEYS, LANE), lambda i: (0, i, 0, 0))
    shape = (PEER_HEADS, n // LANE, PEER_NKEYS, LANE)
    return pl.pallas_call(
        _route_kernel,
        grid=(n // tm,),
        in_specs=[pl.BlockSpec((D_MODEL, tm), lambda i: (0, i)), full(wq_t), full(sk1), full(sk2)],
        out_specs=[out_spec] * 4,
        out_shape=[jax.ShapeDtypeStruct(shape, F32)] * 4,
        scratch_shapes=[pltpu.VMEM((PEER_HEADS * PEER_DKEY, tm), BF)],
        compiler_params=pltpu.CompilerParams(dimension_semantics=("arbitrary",),
                                             vmem_limit_bytes=VMEM_LIMIT),
        name="route",
    )(hb_t, wq_t, sk1, sk2)


def _peer_kernel(x_ref, u_ref, vt_ref, c_ref, p2_ref, e1_ref, e2_ref, o_ref,
                 act0, act1, y0, y1, p2_scr, e2_scr, *, n_chunks):
    s = pl.program_id(1)
    te, tm = y0.shape
    n_a = te // PEER_NKEYS
    groups = PEER_NKEYS // BF_TILE[0]

    @pl.when(s == 0)
    def _():
        o_ref[...] = jnp.zeros_like(o_ref)
        for r in (act0, act1, y0, y1):
            r[...] = jnp.zeros_like(r)
        for hd in range(PEER_HEADS):
            for lc in range(tm // LANE):
                p2_scr[hd, lc] = p2_ref[hd, lc].astype(BF).reshape(groups, *BF_TILE)
                e2_scr[hd, lc] = e2_ref[hd, lc].astype(BF).reshape(groups, *BF_TILE)

    jb = jnp.clip(s - 1, 0, n_chunks - 1)

    def step(act_new, act_old, y_new, y_old):
        o_ref[...] += _dot(vt_ref[0], y_old[...])
        for al in range(n_a):
            a = jb * n_a + al
            for lc in range(tm // LANE):
                w = None
                for hd in range(PEER_HEADS):
                    ca = jnp.broadcast_to(c_ref[hd, lc, pl.ds(a, 1), :], BF_TILE).astype(BF)[None]
                    e1a = jnp.broadcast_to(e1_ref[hd, lc, pl.ds(a, 1), :], BF_TILE).astype(BF)[None]
                    term = jnp.where(p2_scr[hd, lc] < ca, e2_scr[hd, lc], 0.0) * e1a
                    w = term if w is None else w + term
                rows = slice(al * PEER_NKEYS, (al + 1) * PEER_NKEYS)
                cols = slice(lc * LANE, (lc + 1) * LANE)
                y = w * act_old[rows, cols].reshape(groups, *BF_TILE)
                y_new[rows, cols] = y.reshape(PEER_NKEYS, LANE)
        act_new[...] = _gelu(_dot(u_ref[...], x_ref[...])).astype(BF)

    @pl.when(s % 2 == 0)
    def _():
        step(act0, act1, y1, y0)

    @pl.when(s % 2 == 1)
    def _():
        step(act1, act0, y0, y1)


def _peer(hb_t, u_b, vt_b, c, p2, e1, e2, tm=1024):
    n = hb_t.shape[1]
    n_chunks, _, te = vt_b.shape
    last = n_chunks - 1
    rt = pl.BlockSpec((PEER_HEADS, tm // LANE, PEER_NKEYS, LANE), lambda i, s: (0, i, 0, 0),
                      pipeline_mode=pl.Buffered(1))
    packed = pltpu.VMEM((PEER_HEADS, tm // LANE, PEER_NKEYS // BF_TILE[0]) + BF_TILE, BF)
    chunk = pltpu.VMEM((te, tm), BF)
    return pl.pallas_call(
        functools.partial(_peer_kernel, n_chunks=n_chunks),
        grid=(n // tm, n_chunks + 2),
        in_specs=[pl.BlockSpec((D_MODEL, tm), lambda i, s: (0, i)),
                  pl.BlockSpec((te, D_MODEL), lambda i, s: (jnp.minimum(s, last), 0)),
                  pl.BlockSpec((1, D_MODEL, te), lambda i, s: (jnp.clip(s - 2, 0, last), 0, 0)),
                  rt, rt, rt, rt],
        out_specs=pl.BlockSpec((D_MODEL, tm), lambda i, s: (0, i)),
        out_shape=jax.ShapeDtypeStruct((D_MODEL, n), F32),
        scratch_shapes=[chunk, chunk, chunk, chunk, packed, packed],
        compiler_params=pltpu.CompilerParams(dimension_semantics=("arbitrary", "arbitrary"),
                                             vmem_limit_bytes=VMEM_LIMIT),
        name="peer",
    )(hb_t, u_b, vt_b, c, p2, e1, e2)


def _final_kernel(ft_ref, h_ref, p_ref, wg_ref, wp_ref, g_ref, b_ref, o_ref):
    ffn = ft_ref[...].T
    h = _layer_norm(DEEPNORM_ALPHA * h_ref[...] + ffn, g_ref[...], b_ref[...])
    gate = jax.nn.sigmoid(_dot(h.astype(BF), wg_ref[...]))
    o_ref[...] = h + gate * _dot(p_ref[...].astype(BF), wp_ref[...])


def _final(ffn_t, h1, p2d, wg, wp, g, b, tm=256):
    n = h1.shape[0]
    full = lambda a: pl.BlockSpec(a.shape, lambda i: (0, 0))
    return pl.pallas_call(
        _final_kernel,
        grid=(n // tm,),
        in_specs=[pl.BlockSpec((D_MODEL, tm), lambda i: (0, i)),
                  pl.BlockSpec((tm, D_MODEL), lambda i: (i, 0)),
                  pl.BlockSpec((tm, PLE_DIM), lambda i: (i, 0)),
                  full(wg), full(wp), full(g), full(b)],
        out_specs=pl.BlockSpec((tm, D_MODEL), lambda i: (i, 0)),
        out_shape=jax.ShapeDtypeStruct((n, D_MODEL), F32),
        compiler_params=pltpu.CompilerParams(dimension_semantics=("arbitrary",),
                                             vmem_limit_bytes=VMEM_LIMIT),
        name="final",
    )(ffn_t, h1, p2d, wg, wp, g, b)


def _arrange_w_in(w_in):
    cuts = [NSA_WIDTH, 6 * NSA_KV, 3 * NSA_HEADS, 2 * DIFF_QK, 2 * DIFF_QK, DIFF_WIDTH, D_MODEL, D_MODEL]
    offs = [0]
    for c in cuts:
        offs.append(offs[-1] + c)
    w_qn, w_kv, w_g, w_qd, w_kd, w_vd, w_ga, w_gb = [w_in[:, offs[k]:offs[k + 1]] for k in range(8)]
    d = w_in.shape[0]

    def widen(w):
        w = w.reshape(d, -1, HEAD_DIM)
        return jnp.pad(w, ((0, 0), (0, 0), (0, LANE - HEAD_DIM))).reshape(d, -1)

    kv = w_kv.reshape(d, 6, NSA_GROUPS, HEAD_DIM)
    both = lambda a, b: jnp.stack([kv[:, a], kv[:, b]], axis=2).reshape(d, NSA_GROUPS * LANE)
    w_g = w_g.reshape(d, NSA_GROUPS, 3 * NSA_HPG)
    w_g = jnp.pad(w_g, ((0, 0), (0, 0), (0, GATE_ROWS - 3 * NSA_HPG))).reshape(d, NSA_GROUPS * GATE_ROWS)
    pair = lambda w: w.reshape(d, 2, DIFF_HEADS, DIFF_DH).transpose(0, 2, 1, 3).reshape(d, 2 * DIFF_QK)
    cols = {"ksel": widen(kv[:, 2].reshape(d, NSA_KV)), "kwin": widen(kv[:, 4].reshape(d, NSA_KV)),
            "vsw": both(3, 5), "kvc": both(0, 1), "kd": widen(pair(w_kd)), "vd": w_vd, "ga": w_ga, "gb": w_gb}
    rows = {"qn": widen(w_qn * (HEAD_DIM ** -0.5)), "g": w_g,
            "qd": widen(pair(w_qd) * (DIFF_DH ** -0.5))}
    w_all = jnp.concatenate([cols[name] for name, _, _ in PROJ_COLS], axis=1).astype(BF)
    wt_all = jnp.concatenate([rows[name] for name, _, _ in PROJ_ROWS], axis=1).T.astype(BF)
    return w_all, wt_all


def _query_aug():
    slot = jnp.arange(LANE)
    on = ((slot == AUG) | (slot == AUG + 1)).astype(F32)
    nsa = jnp.exp2(-(jnp.arange(NSA_HEADS) + 1.0))
    dif = jnp.repeat(jnp.exp2(-(8.0 / DIFF_HEADS) * (jnp.arange(DIFF_HEADS) + 1.0)), 2)
    parts = {"qn": (nsa[:, None] * on).reshape(-1), "g": jnp.zeros((NSA_GROUPS * GATE_ROWS,), F32),
             "qd": (dif[:, None] * on).reshape(-1)}
    return jnp.concatenate([parts[name] for name, _, _ in PROJ_ROWS]).reshape(PROJ_HEIGHT, 1)


def _sel_weights_t():
    c0 = jnp.arange(N_CMP)[None, :] * CMP_STRIDE
    s0 = jnp.arange(LANE)[:, None] * SEL_BLOCK
    ov = jnp.minimum(c0 + CMP_BLOCK, s0 + SEL_BLOCK) - jnp.maximum(c0, s0)
    w = jnp.clip(ov, 0, None).astype(F32) / CMP_BLOCK
    return jnp.where(jnp.arange(LANE)[:, None] < N_SEL, w, 0.0).astype(BF)


def _token_mixers(x2, batch, seq, w_in, cmp_pos_k, cmp_pos_v, cmp_k_w1, cmp_k_w2, cmp_v_w1, cmp_v_w2,
                  lam_q1, lam_k1, lam_q2, lam_k2, diff_norm_g):
    w_all, wt_all = _arrange_w_in(w_in)
    ksel, kwin, vsw, kvc, kd, vd, ga, gb, qn_t, gates_t, qd_t = _proj(x2, w_all, wt_all, _query_aug(), seq)
    r = kvc.reshape(batch, seq, NSA_GROUPS, 2, HEAD_DIM).transpose(0, 2, 3, 1, 4)
    r = r.reshape(batch, NSA_GROUPS, 2, seq // CMP_STRIDE, CMP_STRIDE * HEAD_DIM)
    pos2 = lambda pe: pe.reshape(2, CMP_STRIDE * HEAD_DIM)
    cmp = _compress(r, pos2(cmp_pos_k), pos2(cmp_pos_v), cmp_k_w1.astype(BF), cmp_k_w2.astype(BF),
                    cmp_v_w1.astype(BF), cmp_v_w2.astype(BF))
    o_nsa = _nsa(qn_t, ksel, kwin, vsw, cmp, gates_t, _sel_weights_t(), batch, seq)
    lamv = jnp.stack([lam_q1, lam_k1, lam_q2, lam_k2]).astype(F32)
    o_diff = _diff(qd_t, kd, vd, lamv, diff_norm_g.reshape(DIFF_VDIM, 1).astype(F32), batch, seq)
    return o_nsa, o_diff, ga, gb


def kernel(x, p, w_in, cmp_pos_k, cmp_pos_v, cmp_k_w1, cmp_k_w2, cmp_v_w1, cmp_v_w2, lam_q1, lam_k1, lam_q2, lam_k2, diff_norm_g, w_branch_nsa, w_branch_diff, w_out, ln1_g, ln1_b, peer_wq, peer_subkeys1, peer_subkeys2, peer_u, peer_v, ln2_g, ln2_b, ple_w_proj, ple_w_gate):
    batch, seq, d = x.shape
    assert seq == N_CMP * CMP_STRIDE and seq == N_SEL * SEL_BLOCK and d == D_MODEL
    x2 = x.reshape(batch * seq, d)
    row = lambda v: v.reshape(1, -1).astype(F32)
    o_nsa, o_diff, ga, gb = _token_mixers(
        x2, batch, seq, w_in[0], cmp_pos_k[0], cmp_pos_v[0], cmp_k_w1[0], cmp_k_w2[0], cmp_v_w1[0],
        cmp_v_w2[0], lam_q1[0], lam_k1[0], lam_q2[0], lam_k2[0], diff_norm_g[0])
    h1, h1b = _merge(o_nsa, o_diff, ga, gb, x2, w_branch_nsa[0].astype(BF), w_branch_diff[0].astype(BF),
                     w_out[0].astype(BF), row(ln1_g[0]), row(ln1_b[0]))
    c, p2, e1, e2 = _route(h1b, peer_wq[0].T.astype(BF), peer_subkeys1[0].astype(BF),
                           peer_subkeys2[0].astype(BF))
    vt = peer_v[0].astype(BF).reshape(PEER_NEXPERTS // PEER_CHUNK, PEER_CHUNK, d).transpose(0, 2, 1)
    ffn_t = _peer(h1b, peer_u[0].astype(BF), vt, c, p2, e1, e2)
    out = _final(ffn_t, h1, p[0].reshape(batch * seq, PLE_DIM), ple_w_gate[0].astype(BF),
                 ple_w_proj[0].astype(BF), row(ln2_g[0]), row(ln2_b[0]))
    return out.reshape(batch, seq, d)
```

```python
import functools
import math

import jax
import jax.numpy as jnp
from jax import lax
from jax.experimental import pallas as pl
from jax.experimental.pallas import tpu as pltpu

D_MODEL = 1024
DEPTH = 1
PLE_DIM = 256

NSA_HEADS = 8
NSA_GROUPS = 2
NSA_HPG = NSA_HEADS // NSA_GROUPS
HEAD_DIM = 64
CMP_BLOCK = 32
CMP_STRIDE = 16
SEL_BLOCK = 64
SEL_TOPK = 8
WINDOW = 512
SEL_FORCE = 1.0e4

DIFF_HEADS = 4
DIFF_DH = 64
DIFF_VDIM = 2 * DIFF_DH

PEER_HEADS = 8
PEER_NKEYS = 128
PEER_NEXPERTS = PEER_NKEYS * PEER_NKEYS
PEER_DKEY = 256
PEER_TOPK = 16

NSA_WIDTH = NSA_HEADS * HEAD_DIM
NSA_KV = NSA_GROUPS * HEAD_DIM
DIFF_QK = DIFF_HEADS * DIFF_DH
DIFF_WIDTH = DIFF_HEADS * DIFF_VDIM

DEEPNORM_ALPHA = (2.0 * DEPTH) ** 0.25
LN_EPS = 1e-5

BF = jnp.bfloat16
F32 = jnp.float32
I32 = jnp.int32

LANE = 128
SUBLANES = 8
BF_TILE = (16, LANE)
Q_TILE_NSA = 256
Q_TILE_DIFF = 512
N_CMP = 128
N_SEL = 32
PEER_CHUNK = 512
VMEM_LIMIT = 56 * 1024 * 1024

AUG = HEAD_DIM
SEL_LANE0 = AUG + 8
M_INIT = -1.0e30
MASKED = -(2.0 ** 100)
GATE_ROWS = 16

PROJ_COLS = (("ksel", NSA_GROUPS * LANE, BF), ("kwin", NSA_GROUPS * LANE, BF), ("vsw", NSA_GROUPS * LANE, BF),
             ("kvc", NSA_GROUPS * LANE, BF), ("kd", 2 * DIFF_HEADS * LANE, BF), ("vd", DIFF_WIDTH, BF),
             ("ga", D_MODEL, BF), ("gb", D_MODEL, BF))
PROJ_WIDTH = sum(w for _, w, _ in PROJ_COLS)
PROJ_ROWS = (("qn", NSA_HEADS * LANE, BF), ("g", NSA_GROUPS * GATE_ROWS, F32), ("qd", 2 * DIFF_HEADS * LANE, BF))
PROJ_HEIGHT = sum(r for _, r, _ in PROJ_ROWS)


def _dot(a, b):
    return jnp.dot(a, b, preferred_element_type=F32)


def _dot_nt(a, b):
    return lax.dot_general(a, b, (((1,), (1,)), ((), ())), preferred_element_type=F32)


def _dot_tn(a, b):
    return lax.dot_general(a, b, (((0,), (0,)), ((), ())), preferred_element_type=F32)


def _gelu(x):
    return 0.5 * x * (1.0 + lax.erf(x * (2.0 ** -0.5)))


def _layer_norm(y, g, b):
    mu = jnp.mean(y, axis=-1, keepdims=True)
    d = y - mu
    var = jnp.mean(d * d, axis=-1, keepdims=True)
    return d * lax.rsqrt(var + LN_EPS) * g + b


def _split_bf16(v):
    hi = v.astype(BF).astype(F32)
    return hi, v - hi


def _proj_kernel(x_ref, w_ref, wt_ref, qaug_ref, *o_refs, seq):
    i = pl.program_id(0)
    tm = x_ref.shape[0]
    x = x_ref[...].astype(BF)
    t = (i * tm + lax.broadcasted_iota(I32, (tm, LANE), 0)) & (seq - 1)
    lane = lax.broadcasted_iota(I32, (tm, LANE), 1)
    hi, lo = _split_bf16(t.astype(F32))
    k_aug = jnp.where(lane == AUG, hi, jnp.where(lane == AUG + 1, lo, 0.0))
    k_aug_sel = k_aug + jnp.where(lane - SEL_LANE0 == (t >> 6), 1.0, 0.0)
    extra = {"ksel": lambda: jnp.concatenate([k_aug_sel] * NSA_GROUPS, axis=1),
             "kwin": lambda: jnp.concatenate([k_aug] * NSA_GROUPS, axis=1),
             "kd": lambda: jnp.concatenate([k_aug] * (2 * DIFF_HEADS), axis=1)}
    c0 = 0
    for (name, width, dt), o_ref in zip(PROJ_COLS, o_refs):
        v = _dot(x, w_ref[:, c0:c0 + width])
        c0 += width
        if name in extra:
            v = v + extra[name]()
        if name in ("ga", "gb"):
            v = jax.nn.sigmoid(v)
        o_ref[...] = v.astype(dt)
    r0 = 0
    for (name, rows, dt), o_ref in zip(PROJ_ROWS, o_refs[len(PROJ_COLS):]):
        v = _dot_nt(wt_ref[r0:r0 + rows, :], x) + qaug_ref[r0:r0 + rows, :]
        r0 += rows
        if name == "g":
            v = jax.nn.sigmoid(v)
        o_ref[...] = v.astype(dt)


def _proj(x2, w_all, wt_all, qaug, seq, tm=256):
    n = x2.shape[0]
    assert seq & (seq - 1) == 0 and seq % tm == 0
    full = lambda a: pl.BlockSpec(a.shape, lambda i: (0, 0))
    return pl.pallas_call(
        functools.partial(_proj_kernel, seq=seq),
        grid=(n // tm,),
        in_specs=[pl.BlockSpec((tm, D_MODEL), lambda i: (i, 0)), full(w_all), full(wt_all), full(qaug)],
        out_specs=([pl.BlockSpec((tm, w), lambda i: (i, 0)) for _, w, _ in PROJ_COLS]
                   + [pl.BlockSpec((r, tm), lambda i: (0, i)) for _, r, _ in PROJ_ROWS]),
        out_shape=([jax.ShapeDtypeStruct((n, w), dt) for _, w, dt in PROJ_COLS]
                   + [jax.ShapeDtypeStruct((r, n), dt) for _, r, dt in PROJ_ROWS]),
        compiler_params=pltpu.CompilerParams(dimension_semantics=("arbitrary",),
                                             vmem_limit_bytes=VMEM_LIMIT),
        name="proj",
    )(x2, w_all, wt_all, qaug)


def _compress_kernel(r_ref, posk_ref, posv_ref, wk1_ref, wk2_ref, wv1_ref, wv2_ref, o_ref):
    half = CMP_STRIDE * HEAD_DIM
    for idx, (pos_ref, w1_ref, w2_ref) in enumerate(((posk_ref, wk1_ref, wk2_ref),
                                                     (posv_ref, wv1_ref, wv2_ref))):
        r = r_ref[0, 0, idx].astype(F32)
        ra = (r + pos_ref[0:1, :]).astype(BF)
        rb = (r + pos_ref[1:2, :]).astype(BF)
        p1 = _dot(ra, w1_ref[0:half, :])
        p2 = _dot(rb, w1_ref[half:2 * half, :])
        hid = p1 + pltpu.roll(p2, N_CMP - 1, 0)
        c = _dot(_gelu(hid).astype(BF), w2_ref[...])
        o_ref[0, 0, :, idx * LANE:idx * LANE + HEAD_DIM] = c.astype(BF)
    lane = lax.broadcasted_iota(I32, (N_CMP, HEAD_DIM), 1)
    c_end = lax.broadcasted_iota(I32, (N_CMP, HEAD_DIM), 0) * CMP_STRIDE + (CMP_BLOCK - 1)
    hi, lo = _split_bf16(c_end.astype(F32))
    o_ref[0, 0, :, AUG:LANE] = jnp.where(lane == 0, hi, jnp.where(lane == 1, lo, 0.0)).astype(BF)
    o_ref[0, 0, :, LANE + HEAD_DIM:2 * LANE] = jnp.zeros((N_CMP, HEAD_DIM), BF)


def _compress(r, posk, posv, wk1, wk2, wv1, wv2):
    b = r.shape[0]
    full = lambda a: pl.BlockSpec(a.shape, lambda i, j: (0,) * a.ndim)
    return pl.pallas_call(
        _compress_kernel,
        grid=(b, NSA_GROUPS),
        in_specs=[pl.BlockSpec((1, 1, 2, N_CMP, CMP_STRIDE * HEAD_DIM), lambda i, j: (i, j, 0, 0, 0)),
                  full(posk), full(posv), full(wk1), full(wk2), full(wv1), full(wv2)],
        out_specs=pl.BlockSpec((1, 1, N_CMP, 2 * LANE), lambda i, j: (i, j, 0, 0)),
        out_shape=jax.ShapeDtypeStruct((b, NSA_GROUPS, N_CMP, 2 * LANE), BF),
        compiler_params=pltpu.CompilerParams(dimension_semantics=("arbitrary", "arbitrary")),
        name="compress",
    )(r, posk, posv, wk1, wk2, wv1, wv2)


def _flash_step(carry, s, v):
    m, l, acc = carry
    m_new = jnp.maximum(m, jnp.max(s, axis=0, keepdims=True))
    a = jnp.exp(m - m_new)
    p = jnp.exp(s - m_new)
    l = a * l + jnp.sum(p, axis=0, keepdims=True)
    acc = a * acc + _dot_tn(v, p.astype(BF))
    return m_new, l, acc


def _flash_init(rows, dv):
    return (jnp.full((1, rows), M_INIT, F32), jnp.zeros((1, rows), F32), jnp.zeros((dv, rows), F32))


def _flash_out(carry):
    _, l, acc = carry
    return acc / jnp.maximum(l, 1e-30)


def _nsa_kernel(qt_ref, ksel_ref, kwin_ref, vsw_ref, cmp_ref, gt_ref, selwt_ref, o_ref):
    i = pl.program_id(2)
    tq = ck = Q_TILE_NSA
    rows = NSA_HPG * tq
    qt = qt_ref[...]
    q4 = jnp.concatenate([qt[h * LANE:(h + 1) * LANE, :] for h in range(NSA_HPG)], axis=1)
    t_lane = i * tq + (lax.broadcasted_iota(I32, (1, rows), 1) & (tq - 1))
    key_col = lax.broadcasted_iota(I32, (ck, 1), 0)
    causal = (i * ck + key_col) <= t_lane

    kc = cmp_ref[0, 0, :, 0:LANE]
    vc = cmp_ref[0, 0, :, LANE:LANE + HEAD_DIM]
    c_end = lax.broadcasted_iota(I32, (N_CMP, 1), 0) * CMP_STRIDE + (CMP_BLOCK - 1)
    s = jnp.where(c_end <= t_lane, _dot(kc, q4), MASKED)
    m = jnp.maximum(jnp.max(s, axis=0, keepdims=True), M_INIT)
    e = jnp.exp(s - m)
    p_cmp = e / jnp.maximum(jnp.sum(e, axis=0, keepdims=True), 1e-30)
    o_cmp = _dot_tn(vc, p_cmp.astype(BF))

    psum = p_cmp[:, 0:tq] + p_cmp[:, tq:2 * tq] + p_cmp[:, 2 * tq:3 * tq] + p_cmp[:, 3 * tq:4 * tq]
    p_hi = psum.astype(BF)
    p_lo = (psum - p_hi.astype(F32)).astype(BF)
    selwt = selwt_ref[...]
    imp = (_dot(selwt, p_hi) + _dot(selwt, p_lo))[0:N_SEL]
    jb = lax.broadcasted_iota(I32, (N_SEL, tq), 0)
    tq_pos = i * tq + lax.broadcasted_iota(I32, (N_SEL, tq), 1)
    cur = tq_pos >> 6
    forced = (jb == 0) | (jb == cur) | (jb == cur - 1)
    score = jnp.where(forced, SEL_FORCE, jnp.where(jb * SEL_BLOCK <= tq_pos, imp, -SEL_FORCE))
    unsel = jnp.full((N_SEL, tq), MASKED, F32)
    for _ in range(SEL_TOPK):
        mx = jnp.max(score, axis=0, keepdims=True)
        idx = jnp.min(jnp.where(score == mx, jb, N_SEL), axis=0, keepdims=True)
        hit = jb == idx
        unsel = jnp.where(hit, 0.0, unsel)
        score = jnp.where(hit, -jnp.inf, score)
    neg = jnp.concatenate([jnp.zeros((SEL_LANE0, tq), F32), unsel,
                           jnp.zeros((LANE - SEL_LANE0 - N_SEL, tq), F32)], axis=0)
    q4s = (q4.astype(F32) + jnp.concatenate([neg] * NSA_HPG, axis=1)).astype(BF)

    def chunk(ref, j, c0, width):
        start = pl.multiple_of(j * ck, ck)
        return ref[pl.ds(start, ck), c0:c0 + width]

    def sel_body(j, carry):
        return _flash_step(carry, _dot(chunk(ksel_ref, j, 0, LANE), q4s), chunk(vsw_ref, j, 0, HEAD_DIM))

    carry = lax.fori_loop(0, i, sel_body, _flash_init(rows, HEAD_DIM))
    s = jnp.where(causal, _dot(chunk(ksel_ref, i, 0, LANE), q4s), MASKED)
    o_slc = _flash_out(_flash_step(carry, s, chunk(vsw_ref, i, 0, HEAD_DIM)))

    carry = _flash_init(rows, HEAD_DIM)
    j2 = jnp.maximum(i - 2, 0)
    in_window = (t_lane - (j2 * ck + key_col) < WINDOW) & (i >= 2)
    s = jnp.where(in_window, _dot(chunk(kwin_ref, j2, 0, LANE), q4), MASKED)
    carry = _flash_step(carry, s, chunk(vsw_ref, j2, HEAD_DIM, HEAD_DIM))
    j1 = jnp.maximum(i - 1, 0)
    s = jnp.where(i >= 1, _dot(chunk(kwin_ref, j1, 0, LANE), q4), MASKED)
    carry = _flash_step(carry, s, chunk(vsw_ref, j1, HEAD_DIM, HEAD_DIM))
    s = jnp.where(causal, _dot(chunk(kwin_ref, i, 0, LANE), q4), MASKED)
    o_win = _flash_out(_flash_step(carry, s, chunk(vsw_ref, i, HEAD_DIM, HEAD_DIM)))

    gates = gt_ref[...]
    outs = []
    for h in range(NSA_HPG):
        c0 = h * tq
        outs.append(gates[3 * h:3 * h + 1] * o_cmp[:, c0:c0 + tq]
                    + gates[3 * h + 1:3 * h + 2] * o_slc[:, c0:c0 + tq]
                    + gates[3 * h + 2:3 * h + 3] * o_win[:, c0:c0 + tq])
    o_ref[...] = jnp.concatenate(outs, axis=0).T.astype(BF)


def _nsa(qn_t, ksel, kwin, vsw, cmp, gates_t, selwt, batch, seq):
    tq = Q_TILE_NSA
    assert WINDOW == 2 * tq and seq % tq == 0
    nq = seq // tq
    per_seq = pl.BlockSpec((seq, LANE), lambda b, g, i: (b, g))
    return pl.pallas_call(
        _nsa_kernel,
        grid=(batch, NSA_GROUPS, nq),
        in_specs=[pl.BlockSpec((NSA_HPG * LANE, tq), lambda b, g, i: (g, b * nq + i)),
                  per_seq, per_seq, per_seq,
                  pl.BlockSpec((1, 1, N_CMP, 2 * LANE), lambda b, g, i: (b, g, 0, 0)),
                  pl.BlockSpec((GATE_ROWS, tq), lambda b, g, i: (g, b * nq + i)),
                  pl.BlockSpec((LANE, LANE), lambda b, g, i: (0, 0))],
        out_specs=pl.BlockSpec((tq, NSA_HPG * HEAD_DIM), lambda b, g, i: (b * nq + i, g)),
        out_shape=jax.ShapeDtypeStruct((batch * seq, NSA_WIDTH), BF),
        compiler_params=pltpu.CompilerParams(dimension_semantics=("arbitrary",) * 3,
                                             vmem_limit_bytes=VMEM_LIMIT),
        name="nsa",
    )(qn_t, ksel, kwin, vsw, cmp, gates_t, selwt)


def _diff_kernel(qt_ref, k_ref, v_ref, lam_ref, ng_ref, o_ref):
    i = pl.program_id(2)
    tq = ck = Q_TILE_DIFF
    q1 = qt_ref[0:LANE, :]
    q2 = qt_ref[LANE:2 * LANE, :]
    lam_init = 0.8 - 0.6 * math.exp(-0.3 * 0)
    lv = lam_ref[...]
    lam = (jnp.exp(jnp.sum(lv[0:1] * lv[1:2], axis=-1, keepdims=True))
           - jnp.exp(jnp.sum(lv[2:3] * lv[3:4], axis=-1, keepdims=True)) + lam_init)
    t_lane = i * tq + lax.broadcasted_iota(I32, (1, tq), 1)
    causal = (i * ck + lax.broadcasted_iota(I32, (ck, 1), 0)) <= t_lane

    def chunk(j):
        start = pl.multiple_of(j * ck, ck)
        return (k_ref[pl.ds(start, ck), 0:LANE], k_ref[pl.ds(start, ck), LANE:2 * LANE],
                v_ref[pl.ds(start, ck), :])

    def body(j, carry):
        c1, c2 = carry
        k1, k2, v = chunk(j)
        return _flash_step(c1, _dot(k1, q1), v), _flash_step(c2, _dot(k2, q2), v)

    init = (_flash_init(tq, DIFF_VDIM), _flash_init(tq, DIFF_VDIM))
    c1, c2 = lax.fori_loop(0, i, body, init)
    k1, k2, v = chunk(i)
    c1 = _flash_step(c1, jnp.where(causal, _dot(k1, q1), MASKED), v)
    c2 = _flash_step(c2, jnp.where(causal, _dot(k2, q2), MASKED), v)
    o = _flash_out(c1) - lam * _flash_out(c2)
    o = o * lax.rsqrt(jnp.mean(o * o, axis=0, keepdims=True) + 1e-5) * ng_ref[...] * (1.0 - lam_init)
    o_ref[...] = o.T.astype(BF)


def _diff(qd_t, kd, vd, lamv, ng_col, batch, seq):
    tq = Q_TILE_DIFF
    nq = seq // tq
    return pl.pallas_call(
        _diff_kernel,
        grid=(batch, DIFF_HEADS, nq),
        in_specs=[pl.BlockSpec((2 * LANE, tq), lambda b, h, i: (h, b * nq + i)),
                  pl.BlockSpec((seq, 2 * LANE), lambda b, h, i: (b, h)),
                  pl.BlockSpec((seq, DIFF_VDIM), lambda b, h, i: (b, h)),
                  pl.BlockSpec((4, DIFF_DH), lambda b, h, i: (0, 0)),
                  pl.BlockSpec((DIFF_VDIM, 1), lambda b, h, i: (0, 0))],
        out_specs=pl.BlockSpec((tq, DIFF_VDIM), lambda b, h, i: (b * nq + i, h)),
        out_shape=jax.ShapeDtypeStruct((batch * seq, DIFF_WIDTH), BF),
        compiler_params=pltpu.CompilerParams(dimension_semantics=("arbitrary",) * 3,
                                             vmem_limit_bytes=VMEM_LIMIT),
        name="diff",
    )(qd_t, kd, vd, lamv, ng_col)


def _merge_kernel(on_ref, od_ref, ga_ref, gb_ref, x_ref, wn_ref, wd_ref, wo_ref, g_ref, b_ref,
                  h_ref, hb_ref):
    merged = (ga_ref[...].astype(F32) * _dot(on_ref[...], wn_ref[...])
              + gb_ref[...].astype(F32) * _dot(od_ref[...], wd_ref[...]))
    mix = _dot(merged.astype(BF), wo_ref[...])
    h = _layer_norm(DEEPNORM_ALPHA * x_ref[...] + mix, g_ref[...], b_ref[...])
    h_ref[...] = h
    hb_ref[...] = h.T.astype(BF)


def _merge(o_nsa, o_diff, ga, gb, x2, wn, wd, wo, g, b, tm=256):
    n = x2.shape[0]
    tok = lambda w: pl.BlockSpec((tm, w), lambda i: (i, 0))
    full = lambda a: pl.BlockSpec(a.shape, lambda i: (0, 0))
    return pl.pallas_call(
        _merge_kernel,
        grid=(n // tm,),
        in_specs=[tok(NSA_WIDTH), tok(DIFF_WIDTH), tok(D_MODEL), tok(D_MODEL), tok(D_MODEL),
                  full(wn), full(wd), full(wo), full(g), full(b)],
        out_specs=[tok(D_MODEL), pl.BlockSpec((D_MODEL, tm), lambda i: (0, i))],
        out_shape=[jax.ShapeDtypeStruct((n, D_MODEL), F32), jax.ShapeDtypeStruct((D_MODEL, n), BF)],
        compiler_params=pltpu.CompilerParams(dimension_semantics=("arbitrary",),
                                             vmem_limit_bytes=VMEM_LIMIT),
        name="merge",
    )(o_nsa, o_diff, ga, gb, x2, wn, wd, wo, g, b)


N_CAND = PEER_TOPK + 8 * (PEER_TOPK - 1)


def _extract_topk(s, k):
    nrow, w = s.shape
    rows = lax.broadcasted_iota(I32, (nrow, w), 0)
    krow = lax.broadcasted_iota(I32, (k, w), 0)
    pos = jnp.full((nrow, w), float(k), F32)
    vals = jnp.zeros((k, w), F32)
    for r in range(k):
        m = jnp.max(s, axis=0, keepdims=True)
        idx = jnp.min(jnp.where(s == m, rows, nrow), axis=0, keepdims=True)
        hit = rows == idx
        pos = jnp.where(hit, float(r), pos)
        s = jnp.where(hit, -jnp.inf, s)
        vals = jnp.where(krow == r, m, vals)
    return vals, pos


def _sort_pairs(n):
    pairs = []

    def merge(lo, hi, r):
        step = r * 2
        if step < hi - lo:
            merge(lo, hi, step)
            merge(lo + r, hi, step)
            pairs.extend((i, i + r) for i in range(lo + r, hi - r, step))
        else:
            pairs.append((lo, lo + r))

    def sort(lo, hi):
        if hi - lo >= 1:
            mid = lo + (hi - lo) // 2
            sort(lo, mid)
            sort(mid + 1, hi)
            merge(lo, hi, 1)

    sort(0, n - 1)
    return pairs


def _sorted_top16(s):
    n = PEER_TOPK
    v = [s[SUBLANES * i:SUBLANES * (i + 1)] for i in range(n)]
    for i, j in _sort_pairs(n):
        v[i], v[j] = jnp.maximum(v[i], v[j]), jnp.minimum(v[i], v[j])
    for shift in (4, 2, 1):
        other = [pltpu.roll(x, shift, 0) for x in v]
        t = [jnp.maximum(v[i], other[n - 1 - i]) for i in range(n)]
        d = n // 2
        while d:
            for i in range(n):
                if not i & d:
                    t[i], t[i + d] = jnp.maximum(t[i], t[i + d]), jnp.minimum(t[i], t[i + d])
            d //= 2
        v = t
    return v


def _route_kernel(h_ref, wq_ref, sk1_ref, sk2_ref, c_ref, p2_ref, e1_ref, e2_ref, q_scr):
    t = h_ref.shape[1]
    q_scr[...] = _dot(wq_ref[...], h_ref[...]).astype(BF)
    half = PEER_DKEY // 2

    crow = lax.broadcasted_iota(I32, (N_CAND, LANE), 0)
    ci = jnp.where(crow < PEER_TOPK, 0, ((crow - PEER_TOPK) >> 3) + 1)
    cj = jnp.where(crow < PEER_TOPK, crow, (crow - PEER_TOPK) & 7)
    cand_ok = (ci + 1) * (cj + 1) <= PEER_TOPK
    krow = lax.broadcasted_iota(I32, (PEER_TOPK, LANE), 0)
    sub = lax.broadcasted_iota(I32, (SUBLANES, LANE), 0)

    def tables(s1, s2):
        v1, pos1 = _extract_topk(s1, PEER_TOPK)
        v2, pos2 = _extract_topk(s2, PEER_TOPK)
        pieces = [v1[0:1] + v2]
        for r in range(1, PEER_TOPK):
            pieces.append(v1[r:r + 1] + v2[0:8])
        cand = jnp.where(cand_ok, jnp.concatenate(pieces, axis=0), -jnp.inf)
        _, cpos = _extract_topk(cand, PEER_TOPK)
        picked = cpos < PEER_TOPK
        top = v1[0:1] + v2[0:1]
        z = jnp.sum(jnp.where(picked, jnp.exp(cand - top), 0.0), axis=0, keepdims=True)
        pk = picked.astype(F32)
        cnt = jnp.zeros((PEER_TOPK, LANE), F32)
        cnt = jnp.where(krow == 0, jnp.sum(pk[0:PEER_TOPK], axis=0, keepdims=True), cnt)
        for r in range(1, PEER_TOPK):
            lo = PEER_TOPK + 8 * (r - 1)
            cnt = jnp.where(krow == r, jnp.sum(pk[lo:lo + 8], axis=0, keepdims=True), cnt)
        c = jnp.zeros((PEER_NKEYS, LANE), F32)
        for r in range(PEER_TOPK):
            c = jnp.where(pos1 == float(r), cnt[r:r + 1], c)
        return c, pos2, jnp.exp(s1 - v1[0:1]) / z, jnp.exp(s2 - v2[0:1])

    def tables_distinct(s1, s2):
        n = PEER_TOPK
        slabs = lambda s: [s[SUBLANES * i:SUBLANES * (i + 1)] for i in range(PEER_NKEYS // SUBLANES)]
        v1 = _sorted_top16(s1)
        v2 = _sorted_top16(s2)
        col = lambda v, base: functools.reduce(
            lambda acc, k: jnp.where(sub == k, v[base + k], acc), range(1, SUBLANES), v[base])
        v2_lo, v2_hi = col(v2, 0), col(v2, SUBLANES)
        pieces = [v1[0] + v2_lo, v1[0] + v2_hi] + [v1[r] + v2_lo for r in range(1, n)]
        cand = jnp.where(cand_ok, jnp.concatenate(pieces, axis=0), -jnp.inf)
        top = v1[0] + v2[0]
        vals, rest = [], cand
        for _ in range(n):
            m = jnp.max(rest, axis=0, keepdims=True)
            rest = jnp.where(rest == m, -jnp.inf, rest)
            vals.append(m)
        tau = jnp.broadcast_to(vals[n - 1], (SUBLANES, LANE))
        z = functools.reduce(lambda acc, m: acc + jnp.exp(m - top[0:1]), vals[1:], jnp.exp(vals[0] - top[0:1]))
        count = lambda pred: jnp.sum(jnp.where(pred, 1.0, 0.0), axis=0, keepdims=True)
        ok = count(cand >= tau[0:1]) == n
        for k in range(n - 1):
            ok = ok & (vals[k] > vals[k + 1]) & (v1[k][0:1] > v1[k + 1][0:1]) & (v2[k][0:1] > v2[k + 1][0:1])
        ok = ok & (count(s1 >= v1[n - 1][0:1]) == n) & (count(s2 >= v2[n - 1][0:1]) == n)
        v1_rows = jnp.concatenate([v1[0], v1[0]] + v1[1:], axis=0)
        kept = jnp.where(cand >= tau[0:1], v1_rows, jnp.inf)
        th_lo = functools.reduce(jnp.minimum, [kept[SUBLANES * i:SUBLANES * (i + 1)] for i in range(2, n + 1)],
                                 kept[0:SUBLANES])
        th_hi = kept[SUBLANES:2 * SUBLANES]
        theta = ([jnp.broadcast_to(th_lo[j:j + 1], (SUBLANES, LANE)) for j in range(SUBLANES)]
                 + [jnp.broadcast_to(th_hi[j:j + 1], (SUBLANES, LANE)) for j in range(SUBLANES)])
        c, p2 = [], []
        for x1, x2 in zip(slabs(s1), slabs(s2)):
            c.append(functools.reduce(lambda acc, k: acc + jnp.where(x1 >= theta[k], 1.0, 0.0),
                                      range(n), jnp.zeros_like(x1)))
            p2.append(functools.reduce(lambda acc, k: acc + jnp.where(v2[k] > x2, 1.0, 0.0),
                                       range(n), jnp.zeros_like(x2)))
        e1 = jnp.exp(s1 - v1[0][0:1]) / z
        e2 = jnp.exp(s2 - v2[0][0:1])
        return (jnp.concatenate(c, axis=0), jnp.concatenate(p2, axis=0), e1, e2), ok

    def head_body(hd, _):
        base = pl.multiple_of(hd * PEER_DKEY, PEER_DKEY)
        s1_all = _dot(sk1_ref[...], q_scr[pl.ds(base, half), :])
        s2_all = _dot(sk2_ref[...], q_scr[pl.ds(base + half, half), :])
        for lc in range(t // LANE):
            s1 = s1_all[:, lc * LANE:(lc + 1) * LANE]
            s2 = s2_all[:, lc * LANE:(lc + 1) * LANE]

            def store(vals):
                for ref, v in zip((c_ref, p2_ref, e1_ref, e2_ref), vals):
                    ref[hd, lc] = v

            vals, distinct = tables_distinct(s1, s2)
            store(vals)

            @pl.when(jnp.min(jnp.where(distinct, 1.0, 0.0)) < 0.5)
            def _():
                store(tables(s1, s2))
        return 0

    lax.fori_loop(0, PEER_HEADS, head_body, 0)


def _route(hb_t, wq_t, sk1, sk2, tm=256):
    n = hb_t.shape[1]
    full = lambda a: pl.BlockSpec(a.shape, lambda i: (0, 0))
    out_spec = pl.BlockSpec((PEER_HEADS, tm // LANE, PEER_NKEYS, LANE), lambda i: (0, i, 0, 0))
    shape = (PEER_HEADS, n // LANE, PEER_NKEYS, LANE)
    return pl.pallas_call(
        _route_kernel,
        grid=(n // tm,),
        in_specs=[pl.BlockSpec((D_MODEL, tm), lambda i: (0, i)), full(wq_t), full(sk1), full(sk2)],
        out_specs=[out_spec] * 4,
        out_shape=[jax.ShapeDtypeStruct(shape, F32)] * 4,
        scratch_shapes=[pltpu.VMEM((PEER_HEADS * PEER_DKEY, tm), BF)],
        compiler_params=pltpu.CompilerParams(dimension_semantics=("arbitrary",),
                                             vmem_limit_bytes=VMEM_LIMIT),
        name="route",
    )(hb_t, wq_t, sk1, sk2)


def _peer_kernel(x_ref, u_ref, vt_ref, c_ref, p2_ref, e1_ref, e2_ref, o_ref,
                 act0, act1, y0, y1, p2_scr, e2_scr, *, n_chunks):
    s = pl.program_id(1)
    te, tm = y0.shape
    n_a = te // PEER_NKEYS
    groups = PEER_NKEYS // BF_TILE[0]

    @pl.when(s == 0)
    def _():
        o_ref[...] = jnp.zeros_like(o_ref)
        for r in (act0, act1, y0, y1):
            r[...] = jnp.zeros_like(r)
        for hd in range(PEER_HEADS):
            for lc in range(tm // LANE):
                p2_scr[hd, lc] = p2_ref[hd, lc].astype(BF).reshape(groups, *BF_TILE)
                e2_scr[hd, lc] = e2_ref[hd, lc].astype(BF).reshape(groups, *BF_TILE)

    jb = jnp.clip(s - 1, 0, n_chunks - 1)

    def step(act_new, act_old, y_new, y_old):
        o_ref[...] += _dot(vt_ref[0], y_old[...])
        for al in range(n_a):
            a = jb * n_a + al
            for lc in range(tm // LANE):
                w = None
                for hd in range(PEER_HEADS):
                    ca = jnp.broadcast_to(c_ref[hd, lc, pl.ds(a, 1), :], BF_TILE).astype(BF)[None]
                    e1a = jnp.broadcast_to(e1_ref[hd, lc, pl.ds(a, 1), :], BF_TILE).astype(BF)[None]
                    term = jnp.where(p2_scr[hd, lc] < ca, e2_scr[hd, lc], 0.0) * e1a
                    w = term if w is None else w + term
                rows = slice(al * PEER_NKEYS, (al + 1) * PEER_NKEYS)
                cols = slice(lc * LANE, (lc + 1) * LANE)
                y = w * act_old[rows, cols].reshape(groups, *BF_TILE)
                y_new[rows, cols] = y.reshape(PEER_NKEYS, LANE)
        act_new[...] = _gelu(_dot(u_ref[...], x_ref[...])).astype(BF)

    @pl.when(s % 2 == 0)
    def _():
        step(act0, act1, y1, y0)

    @pl.when(s % 2 == 1)
    def _():
        step(act1, act0, y0, y1)


def _peer(hb_t, u_b, vt_b, c, p2, e1, e2, tm=1024):
    n = hb_t.shape[1]
    n_chunks, _, te = vt_b.shape
    last = n_chunks - 1
    rt = pl.BlockSpec((PEER_HEADS, tm // LANE, PEER_NKEYS, LANE), lambda i, s: (0, i, 0, 0),
                      pipeline_mode=pl.Buffered(1))
    packed = pltpu.VMEM((PEER_HEADS, tm // LANE, PEER_NKEYS // BF_TILE[0]) + BF_TILE, BF)
    chunk = pltpu.VMEM((te, tm), BF)
    return pl.pallas_call(
        functools.partial(_peer_kernel, n_chunks=n_chunks),
        grid=(n // tm, n_chunks + 2),
        in_specs=[pl.BlockSpec((D_MODEL, tm), lambda i, s: (0, i)),
                  pl.BlockSpec((te, D_MODEL), lambda i, s: (jnp.minimum(s, last), 0)),
                  pl.BlockSpec((1, D_MODEL, te), lambda i, s: (jnp.clip(s - 2, 0, last), 0, 0)),
                  rt, rt, rt, rt],
        out_specs=pl.BlockSpec((D_MODEL, tm), lambda i, s: (0, i)),
        out_shape=jax.ShapeDtypeStruct((D_MODEL, n), F32),
        scratch_shapes=[chunk, chunk, chunk, chunk, packed, packed],
        compiler_params=pltpu.CompilerParams(dimension_semantics=("arbitrary", "arbitrary"),
                                             vmem_limit_bytes=VMEM_LIMIT),
        name="peer",
    )(hb_t, u_b, vt_b, c, p2, e1, e2)


def _final_kernel(ft_ref, h_ref, p_ref, wg_ref, wp_ref, g_ref, b_ref, o_ref):
    ffn = ft_ref[...].T
    h = _layer_norm(DEEPNORM_ALPHA * h_ref[...] + ffn, g_ref[...], b_ref[...])
    gate = jax.nn.sigmoid(_dot(h.astype(BF), wg_ref[...]))
    o_ref[...] = h + gate * _dot(p_ref[...].astype(BF), wp_ref[...])


def _final(ffn_t, h1, p2d, wg, wp, g, b, tm=256):
    n = h1.shape[0]
    full = lambda a: pl.BlockSpec(a.shape, lambda i: (0, 0))
    return pl.pallas_call(
        _final_kernel,
        grid=(n // tm,),
        in_specs=[pl.BlockSpec((D_MODEL, tm), lambda i: (0, i)),
                  pl.BlockSpec((tm, D_MODEL), lambda i: (i, 0)),
                  pl.BlockSpec((tm, PLE_DIM), lambda i: (i, 0)),
                  full(wg), full(wp), full(g), full(b)],
        out_specs=pl.BlockSpec((tm, D_MODEL), lambda i: (i, 0)),
        out_shape=jax.ShapeDtypeStruct((n, D_MODEL), F32),
        compiler_params=pltpu.CompilerParams(dimension_semantics=("arbitrary",),
                                             vmem_limit_bytes=VMEM_LIMIT),
        name="final",
    )(ffn_t, h1, p2d, wg, wp, g, b)


def _arrange_w_in(w_in):
    cuts = [NSA_WIDTH, 6 * NSA_KV, 3 * NSA_HEADS, 2 * DIFF_QK, 2 * DIFF_QK, DIFF_WIDTH, D_MODEL, D_MODEL]
    offs = [0]
    for c in cuts:
        offs.append(offs[-1] + c)
    w_qn, w_kv, w_g, w_qd, w_kd, w_vd, w_ga, w_gb = [w_in[:, offs[k]:offs[k + 1]] for k in range(8)]
    d = w_in.shape[0]

    def widen(w):
        w = w.reshape(d, -1, HEAD_DIM)
        return jnp.pad(w, ((0, 0), (0, 0), (0, LANE - HEAD_DIM))).reshape(d, -1)

    kv = w_kv.reshape(d, 6, NSA_GROUPS, HEAD_DIM)
    both = lambda a, b: jnp.stack([kv[:, a], kv[:, b]], axis=2).reshape(d, NSA_GROUPS * LANE)
    w_g = w_g.reshape(d, NSA_GROUPS, 3 * NSA_HPG)
    w_g = jnp.pad(w_g, ((0, 0), (0, 0), (0, GATE_ROWS - 3 * NSA_HPG))).reshape(d, NSA_GROUPS * GATE_ROWS)
    pair = lambda w: w.reshape(d, 2, DIFF_HEADS, DIFF_DH).transpose(0, 2, 1, 3).reshape(d, 2 * DIFF_QK)
    cols = {"ksel": widen(kv[:, 2].reshape(d, NSA_KV)), "kwin": widen(kv[:, 4].reshape(d, NSA_KV)),
            "vsw": both(3, 5), "kvc": both(0, 1), "kd": widen(pair(w_kd)), "vd": w_vd, "ga": w_ga, "gb": w_gb}
    rows = {"qn": widen(w_qn * (HEAD_DIM ** -0.5)), "g": w_g,
            "qd": widen(pair(w_qd) * (DIFF_DH ** -0.5))}
    w_all = jnp.concatenate([cols[name] for name, _, _ in PROJ_COLS], axis=1).astype(BF)
    wt_all = jnp.concatenate([rows[name] for name, _, _ in PROJ_ROWS], axis=1).T.astype(BF)
    return w_all, wt_all


def _query_aug():
    slot = jnp.arange(LANE)
    on = ((slot == AUG) | (slot == AUG + 1)).astype(F32)
    nsa = jnp.exp2(-(jnp.arange(NSA_HEADS) + 1.0))
    dif = jnp.repeat(jnp.exp2(-(8.0 / DIFF_HEADS) * (jnp.arange(DIFF_HEADS) + 1.0)), 2)
    parts = {"qn": (nsa[:, None] * on).reshape(-1), "g": jnp.zeros((NSA_GROUPS * GATE_ROWS,), F32),
             "qd": (dif[:, None] * on).reshape(-1)}
    return jnp.concatenate([parts[name] for name, _, _ in PROJ_ROWS]).reshape(PROJ_HEIGHT, 1)


def _sel_weights_t():
    c0 = jnp.arange(N_CMP)[None, :] * CMP_STRIDE
    s0 = jnp.arange(LANE)[:, None] * SEL_BLOCK
    ov = jnp.minimum(c0 + CMP_BLOCK, s0 + SEL_BLOCK) - jnp.maximum(c0, s0)
    w = jnp.clip(ov, 0, None).astype(F32) / CMP_BLOCK
    return jnp.where(jnp.arange(LANE)[:, None] < N_SEL, w, 0.0).astype(BF)


def _token_mixers(x2, batch, seq, w_in, cmp_pos_k, cmp_pos_v, cmp_k_w1, cmp_k_w2, cmp_v_w1, cmp_v_w2,
                  lam_q1, lam_k1, lam_q2, lam_k2, diff_norm_g):
    w_all, wt_all = _arrange_w_in(w_in)
    ksel, kwin, vsw, kvc, kd, vd, ga, gb, qn_t, gates_t, qd_t = _proj(x2, w_all, wt_all, _query_aug(), seq)
    r = kvc.reshape(batch, seq, NSA_GROUPS, 2, HEAD_DIM).transpose(0, 2, 3, 1, 4)
    r = r.reshape(batch, NSA_GROUPS, 2, seq // CMP_STRIDE, CMP_STRIDE * HEAD_DIM)
    pos2 = lambda pe: pe.reshape(2, CMP_STRIDE * HEAD_DIM)
    cmp = _compress(r, pos2(cmp_pos_k), pos2(cmp_pos_v), cmp_k_w1.astype(BF), cmp_k_w2.astype(BF),
                    cmp_v_w1.astype(BF), cmp_v_w2.astype(BF))
    o_nsa = _nsa(qn_t, ksel, kwin, vsw, cmp, gates_t, _sel_weights_t(), batch, seq)
    lamv = jnp.stack([lam_q1, lam_k1, lam_q2, lam_k2]).astype(F32)
    o_diff = _diff(qd_t, kd, vd, lamv, diff_norm_g.reshape(DIFF_VDIM, 1).astype(F32), batch, seq)
    return o_nsa, o_diff, ga, gb


def kernel(x, p, w_in, cmp_pos_k, cmp_pos_v, cmp_k_w1, cmp_k_w2, cmp_v_w1, cmp_v_w2, lam_q1, lam_k1, lam_q2, lam_k2, diff_norm_g, w_branch_nsa, w_branch_diff, w_out, ln1_g, ln1_b, peer_wq, peer_subkeys1, peer_subkeys2, peer_u, peer_v, ln2_g, ln2_b, ple_w_proj, ple_w_gate):
    batch, seq, d = x.shape
    assert seq == N_CMP * CMP_STRIDE and seq == N_SEL * SEL_BLOCK and d == D_MODEL
    x2 = x.reshape(batch * seq, d)
    row = lambda v: v.reshape(1, -1).astype(F32)
    o_nsa, o_diff, ga, gb = _token_mixers(
        x2, batch, seq, w_in[0], cmp_pos_k[0], cmp_pos_v[0], cmp_k_w1[0], cmp_k_w2[0], cmp_v_w1[0],
        cmp_v_w2[0], lam_q1[0], lam_k1[0], lam_q2[0], lam_k2[0], diff_norm_g[0])
    h1, h1b = _merge(o_nsa, o_diff, ga, gb, x2, w_branch_nsa[0].astype(BF), w_branch_diff[0].astype(BF),
                     w_out[0].astype(BF), row(ln1_g[0]), row(ln1_b[0]))
    c, p2, e1, e2 = _route(h1b, peer_wq[0].T.astype(BF), peer_subkeys1[0].astype(BF),
                           peer_subkeys2[0].astype(BF))
    vt = peer_v[0].astype(BF).reshape(PEER_NEXPERTS // PEER_CHUNK, PEER_CHUNK, d).transpose(0, 2, 1)
    ffn_t = _peer(h1b, peer_u[0].astype(BF), vt, c, p2, e1, e2)
    out = _final(ffn_t, h1, p[0].reshape(batch * seq, PLE_DIM), ple_w_gate[0].astype(BF),
                 ple_w_proj[0].astype(BF), row(ln2_g[0]), row(ln2_b[0]))
    return out.reshape(batch, seq, d)
```

```python
import functools
import math

import jax
import jax.numpy as jnp
from jax import lax
from jax.experimental import pallas as pl
from jax.experimental.pallas import tpu as pltpu

D_MODEL = 1024
DEPTH = 1
PLE_DIM = 256

NSA_HEADS = 8
NSA_GROUPS = 2
NSA_HPG = NSA_HEADS // NSA_GROUPS
HEAD_DIM = 64
CMP_BLOCK = 32
CMP_STRIDE = 16
SEL_BLOCK = 64
SEL_TOPK = 8
WINDOW = 512
SEL_FORCE = 1.0e4

DIFF_HEADS = 4
DIFF_DH = 64
DIFF_VDIM = 2 * DIFF_DH

PEER_HEADS = 8
PEER_NKEYS = 128
PEER_NEXPERTS = PEER_NKEYS * PEER_NKEYS
PEER_DKEY = 256
PEER_TOPK = 16

NSA_WIDTH = NSA_HEADS * HEAD_DIM
NSA_KV = NSA_GROUPS * HEAD_DIM
DIFF_QK = DIFF_HEADS * DIFF_DH
DIFF_WIDTH = DIFF_HEADS * DIFF_VDIM

DEEPNORM_ALPHA = (2.0 * DEPTH) ** 0.25
LN_EPS = 1e-5

BF = jnp.bfloat16
F32 = jnp.float32
I32 = jnp.int32

LANE = 128
SUBLANES = 8
BF_TILE = (16, LANE)
Q_TILE_NSA = 256
Q_TILE_DIFF = 1024
N_CMP = 128
N_SEL = 32
PEER_CHUNK = 512
VMEM_LIMIT = 56 * 1024 * 1024

AUG = HEAD_DIM
SEL_LANE0 = AUG + 8
M_INIT = -1.0e30
MASKED = -(2.0 ** 100)
GATE_ROWS = 16

PROJ_COLS = (("ksel", NSA_GROUPS * LANE, BF), ("kwin", NSA_GROUPS * LANE, BF), ("vsw", NSA_GROUPS * LANE, BF),
             ("kvc", NSA_GROUPS * LANE, BF), ("kd", 2 * DIFF_HEADS * LANE, BF), ("vd", DIFF_WIDTH, BF),
             ("ga", D_MODEL, BF), ("gb", D_MODEL, BF))
PROJ_WIDTH = sum(w for _, w, _ in PROJ_COLS)
PROJ_ROWS = (("qn", NSA_HEADS * LANE, BF), ("g", NSA_GROUPS * GATE_ROWS, F32), ("qd", 2 * DIFF_HEADS * LANE, BF))
PROJ_HEIGHT = sum(r for _, r, _ in PROJ_ROWS)


def _dot(a, b):
    return jnp.dot(a, b, preferred_element_type=F32)


def _dot_nt(a, b):
    return lax.dot_general(a, b, (((1,), (1,)), ((), ())), preferred_element_type=F32)


def _dot_tn(a, b):
    return lax.dot_general(a, b, (((0,), (0,)), ((), ())), preferred_element_type=F32)


def _gelu(x):
    return 0.5 * x * (1.0 + lax.erf(x * (2.0 ** -0.5)))


def _layer_norm(y, g, b):
    mu = jnp.mean(y, axis=-1, keepdims=True)
    d = y - mu
    var = jnp.mean(d * d, axis=-1, keepdims=True)
    return d * lax.rsqrt(var + LN_EPS) * g + b


def _split_bf16(v):
    hi = v.astype(BF).astype(F32)
    return hi, v - hi


def _proj_kernel(x_ref, w_ref, wt_ref, qaug_ref, *o_refs, seq):
    i = pl.program_id(0)
    tm = x_ref.shape[0]
    x = x_ref[...].astype(BF)
    t = (i * tm + lax.broadcasted_iota(I32, (tm, LANE), 0)) & (seq - 1)
    lane = lax.broadcasted_iota(I32, (tm, LANE), 1)
    hi, lo = _split_bf16(t.astype(F32))
    k_aug = jnp.where(lane == AUG, hi, jnp.where(lane == AUG + 1, lo, 0.0))
    k_aug_sel = k_aug + jnp.where(lane - SEL_LANE0 == (t >> 6), 1.0, 0.0)
    extra = {"ksel": lambda: jnp.concatenate([k_aug_sel] * NSA_GROUPS, axis=1),
             "kwin": lambda: jnp.concatenate([k_aug] * NSA_GROUPS, axis=1),
             "kd": lambda: jnp.concatenate([k_aug] * (2 * DIFF_HEADS), axis=1)}
    c0 = 0
    for (name, width, dt), o_ref in zip(PROJ_COLS, o_refs):
        v = _dot(x, w_ref[:, c0:c0 + width])
        c0 += width
        if name in extra:
            v = v + extra[name]()
        if name in ("ga", "gb"):
            v = jax.nn.sigmoid(v)
        o_ref[...] = v.astype(dt)
    r0 = 0
    for (name, rows, dt), o_ref in zip(PROJ_ROWS, o_refs[len(PROJ_COLS):]):
        v = _dot_nt(wt_ref[r0:r0 + rows, :], x) + qaug_ref[r0:r0 + rows, :]
        r0 += rows
        if name == "g":
            v = jax.nn.sigmoid(v)
        o_ref[...] = v.astype(dt)


def _proj(x2, w_all, wt_all, qaug, seq, tm=256):
    n = x2.shape[0]
    assert seq & (seq - 1) == 0 and seq % tm == 0
    full = lambda a: pl.BlockSpec(a.shape, lambda i: (0, 0))
    return pl.pallas_call(
        functools.partial(_proj_kernel, seq=seq),
        grid=(n // tm,),
        in_specs=[pl.BlockSpec((tm, D_MODEL), lambda i: (i, 0)), full(w_all), full(wt_all), full(qaug)],
        out_specs=([pl.BlockSpec((tm, w), lambda i: (i, 0)) for _, w, _ in PROJ_COLS]
                   + [pl.BlockSpec((r, tm), lambda i: (0, i)) for _, r, _ in PROJ_ROWS]),
        out_shape=([jax.ShapeDtypeStruct((n, w), dt) for _, w, dt in PROJ_COLS]
                   + [jax.ShapeDtypeStruct((r, n), dt) for _, r, dt in PROJ_ROWS]),
        compiler_params=pltpu.CompilerParams(dimension_semantics=("arbitrary",),
                                             vmem_limit_bytes=VMEM_LIMIT),
        name="proj",
    )(x2, w_all, wt_all, qaug)


def _compress_kernel(r_ref, posk_ref, posv_ref, wk1_ref, wk2_ref, wv1_ref, wv2_ref, o_ref):
    half = CMP_STRIDE * HEAD_DIM
    for idx, (pos_ref, w1_ref, w2_ref) in enumerate(((posk_ref, wk1_ref, wk2_ref),
                                                     (posv_ref, wv1_ref, wv2_ref))):
        r = r_ref[0, 0, idx].astype(F32)
        ra = (r + pos_ref[0:1, :]).astype(BF)
        rb = (r + pos_ref[1:2, :]).astype(BF)
        p1 = _dot(ra, w1_ref[0:half, :])
        p2 = _dot(rb, w1_ref[half:2 * half, :])
        hid = p1 + pltpu.roll(p2, N_CMP - 1, 0)
        c = _dot(_gelu(hid).astype(BF), w2_ref[...])
        o_ref[0, 0, :, idx * LANE:idx * LANE + HEAD_DIM] = c.astype(BF)
    lane = lax.broadcasted_iota(I32, (N_CMP, HEAD_DIM), 1)
    c_end = lax.broadcasted_iota(I32, (N_CMP, HEAD_DIM), 0) * CMP_STRIDE + (CMP_BLOCK - 1)
    hi, lo = _split_bf16(c_end.astype(F32))
    o_ref[0, 0, :, AUG:LANE] = jnp.where(lane == 0, hi, jnp.where(lane == 1, lo, 0.0)).astype(BF)
    o_ref[0, 0, :, LANE + HEAD_DIM:2 * LANE] = jnp.zeros((N_CMP, HEAD_DIM), BF)


def _compress(r, posk, posv, wk1, wk2, wv1, wv2):
    b = r.shape[0]
    full = lambda a: pl.BlockSpec(a.shape, lambda i, j: (0,) * a.ndim)
    return pl.pallas_call(
        _compress_kernel,
        grid=(b, NSA_GROUPS),
        in_specs=[pl.BlockSpec((1, 1, 2, N_CMP, CMP_STRIDE * HEAD_DIM), lambda i, j: (i, j, 0, 0, 0)),
                  full(posk), full(posv), full(wk1), full(wk2), full(wv1), full(wv2)],
        out_specs=pl.BlockSpec((1, 1, N_CMP, 2 * LANE), lambda i, j: (i, j, 0, 0)),
        out_shape=jax.ShapeDtypeStruct((b, NSA_GROUPS, N_CMP, 2 * LANE), BF),
        compiler_params=pltpu.CompilerParams(dimension_semantics=("arbitrary", "arbitrary")),
        name="compress",
    )(r, posk, posv, wk1, wk2, wv1, wv2)


def _flash_step(carry, s, v):
    m, l, acc = carry
    m_new = jnp.maximum(m, jnp.max(s, axis=0, keepdims=True))
    a = jnp.exp(m - m_new)
    p = jnp.exp(s - m_new)
    l = a * l + jnp.sum(p, axis=0, keepdims=True)
    acc = a * acc + _dot_tn(v, p.astype(BF))
    return m_new, l, acc


def _flash_init(rows, dv):
    return (jnp.full((1, rows), M_INIT, F32), jnp.zeros((1, rows), F32), jnp.zeros((dv, rows), F32))


def _flash_out(carry):
    _, l, acc = carry
    return acc / jnp.maximum(l, 1e-30)


def _nsa_kernel(qt_ref, ksel_ref, kwin_ref, vsw_ref, cmp_ref, gt_ref, selwt_ref, o_ref):
    i = pl.program_id(2)
    tq = ck = Q_TILE_NSA
    rows = NSA_HPG * tq
    qt = qt_ref[...]
    q4 = jnp.concatenate([qt[h * LANE:(h + 1) * LANE, :] for h in range(NSA_HPG)], axis=1)
    t_lane = i * tq + (lax.broadcasted_iota(I32, (1, rows), 1) & (tq - 1))
    key_col = lax.broadcasted_iota(I32, (ck, 1), 0)
    causal = (i * ck + key_col) <= t_lane

    kc = cmp_ref[0, 0, :, 0:LANE]
    vc = cmp_ref[0, 0, :, LANE:LANE + HEAD_DIM]
    c_end = lax.broadcasted_iota(I32, (N_CMP, 1), 0) * CMP_STRIDE + (CMP_BLOCK - 1)
    s = jnp.where(c_end <= t_lane, _dot(kc, q4), MASKED)
    m = jnp.maximum(jnp.max(s, axis=0, keepdims=True), M_INIT)
    e = jnp.exp(s - m)
    p_cmp = e / jnp.maximum(jnp.sum(e, axis=0, keepdims=True), 1e-30)
    o_cmp = _dot_tn(vc, p_cmp.astype(BF))

    psum = p_cmp[:, 0:tq] + p_cmp[:, tq:2 * tq] + p_cmp[:, 2 * tq:3 * tq] + p_cmp[:, 3 * tq:4 * tq]
    p_hi = psum.astype(BF)
    p_lo = (psum - p_hi.astype(F32)).astype(BF)
    selwt = selwt_ref[...]
    imp = (_dot(selwt, p_hi) + _dot(selwt, p_lo))[0:N_SEL]
    jb = lax.broadcasted_iota(I32, (N_SEL, tq), 0)
    tq_pos = i * tq + lax.broadcasted_iota(I32, (N_SEL, tq), 1)
    cur = tq_pos >> 6
    forced = (jb == 0) | (jb == cur) | (jb == cur - 1)
    score = jnp.where(forced, SEL_FORCE, jnp.where(jb * SEL_BLOCK <= tq_pos, imp, -SEL_FORCE))
    unsel = jnp.full((N_SEL, tq), MASKED, F32)
    for _ in range(SEL_TOPK):
        mx = jnp.max(score, axis=0, keepdims=True)
        idx = jnp.min(jnp.where(score == mx, jb, N_SEL), axis=0, keepdims=True)
        hit = jb == idx
        unsel = jnp.where(hit, 0.0, unsel)
        score = jnp.where(hit, -jnp.inf, score)
    neg = jnp.concatenate([jnp.zeros((SEL_LANE0, tq), F32), unsel,
                           jnp.zeros((LANE - SEL_LANE0 - N_SEL, tq), F32)], axis=0)
    q4s = (q4.astype(F32) + jnp.concatenate([neg] * NSA_HPG, axis=1)).astype(BF)

    def chunk(ref, j, c0, width):
        start = pl.multiple_of(j * ck, ck)
        return ref[pl.ds(start, ck), c0:c0 + width]

    def sel_body(j, carry):
        return _flash_step(carry, _dot(chunk(ksel_ref, j, 0, LANE), q4s), chunk(vsw_ref, j, 0, HEAD_DIM))

    carry = lax.fori_loop(0, i, sel_body, _flash_init(rows, HEAD_DIM))
    s = jnp.where(causal, _dot(chunk(ksel_ref, i, 0, LANE), q4s), MASKED)
    o_slc = _flash_out(_flash_step(carry, s, chunk(vsw_ref, i, 0, HEAD_DIM)))

    carry = _flash_init(rows, HEAD_DIM)
    j2 = jnp.maximum(i - 2, 0)
    in_window = (t_lane - (j2 * ck + key_col) < WINDOW) & (i >= 2)
    s = jnp.where(in_window, _dot(chunk(kwin_ref, j2, 0, LANE), q4), MASKED)
    carry = _flash_step(carry, s, chunk(vsw_ref, j2, HEAD_DIM, HEAD_DIM))
    j1 = jnp.maximum(i - 1, 0)
    s = jnp.where(i >= 1, _dot(chunk(kwin_ref, j1, 0, LANE), q4), MASKED)
    carry = _flash_step(carry, s, chunk(vsw_ref, j1, HEAD_DIM, HEAD_DIM))
    s = jnp.where(causal, _dot(chunk(kwin_ref, i, 0, LANE), q4), MASKED)
    o_win = _flash_out(_flash_step(carry, s, chunk(vsw_ref, i, HEAD_DIM, HEAD_DIM)))

    gates = gt_ref[...]
    outs = []
    for h in range(NSA_HPG):
        c0 = h * tq
        outs.append(gates[3 * h:3 * h + 1] * o_cmp[:, c0:c0 + tq]
                    + gates[3 * h + 1:3 * h + 2] * o_slc[:, c0:c0 + tq]
                    + gates[3 * h + 2:3 * h + 3] * o_win[:, c0:c0 + tq])
    o_ref[...] = jnp.concatenate(outs, axis=0).T.astype(BF)


def _nsa(qn_t, ksel, kwin, vsw, cmp, gates_t, selwt, batch, seq):
    tq = Q_TILE_NSA
    assert WINDOW == 2 * tq and seq % tq == 0
    nq = seq // tq
    per_seq = pl.BlockSpec((seq, LANE), lambda b, g, i: (b, g))
    return pl.pallas_call(
        _nsa_kernel,
        grid=(batch, NSA_GROUPS, nq),
        in_specs=[pl.BlockSpec((NSA_HPG * LANE, tq), lambda b, g, i: (g, b * nq + i)),
                  per_seq, per_seq, per_seq,
                  pl.BlockSpec((1, 1, N_CMP, 2 * LANE), lambda b, g, i: (b, g, 0, 0)),
                  pl.BlockSpec((GATE_ROWS, tq), lambda b, g, i: (g, b * nq + i)),
                  pl.BlockSpec((LANE, LANE), lambda b, g, i: (0, 0))],
        out_specs=pl.BlockSpec((tq, NSA_HPG * HEAD_DIM), lambda b, g, i: (b * nq + i, g)),
        out_shape=jax.ShapeDtypeStruct((batch * seq, NSA_WIDTH), BF),
        compiler_params=pltpu.CompilerParams(dimension_semantics=("arbitrary",) * 3,
                                             vmem_limit_bytes=VMEM_LIMIT),
        name="nsa",
    )(qn_t, ksel, kwin, vsw, cmp, gates_t, selwt)


def _diff_kernel(qt_ref, k_ref, v_ref, lam_ref, ng_ref, o_ref):
    i = pl.program_id(2)
    tq = ck = Q_TILE_DIFF
    q1 = qt_ref[0:LANE, :]
    q2 = qt_ref[LANE:2 * LANE, :]
    lam_init = 0.8 - 0.6 * math.exp(-0.3 * 0)
    lv = lam_ref[...]
    lam = (jnp.exp(jnp.sum(lv[0:1] * lv[1:2], axis=-1, keepdims=True))
           - jnp.exp(jnp.sum(lv[2:3] * lv[3:4], axis=-1, keepdims=True)) + lam_init)
    t_lane = i * tq + lax.broadcasted_iota(I32, (1, tq), 1)
    causal = (i * ck + lax.broadcasted_iota(I32, (ck, 1), 0)) <= t_lane

    def chunk(j):
        start = pl.multiple_of(j * ck, ck)
        return (k_ref[pl.ds(start, ck), 0:LANE], k_ref[pl.ds(start, ck), LANE:2 * LANE],
                v_ref[pl.ds(start, ck), :])

    def body(j, carry):
        c1, c2 = carry
        k1, k2, v = chunk(j)
        return _flash_step(c1, _dot(k1, q1), v), _flash_step(c2, _dot(k2, q2), v)

    init = (_flash_init(tq, DIFF_VDIM), _flash_init(tq, DIFF_VDIM))
    c1, c2 = lax.fori_loop(0, i, body, init)
    k1, k2, v = chunk(i)
    c1 = _flash_step(c1, jnp.where(causal, _dot(k1, q1), MASKED), v)
    c2 = _flash_step(c2, jnp.where(causal, _dot(k2, q2), MASKED), v)
    o = _flash_out(c1) - lam * _flash_out(c2)
    o = o * lax.rsqrt(jnp.mean(o * o, axis=0, keepdims=True) + 1e-5) * ng_ref[...] * (1.0 - lam_init)
    o_ref[...] = o.T.astype(BF)


def _diff(qd_t, kd, vd, lamv, ng_col, batch, seq):
    tq = Q_TILE_DIFF
    nq = seq // tq
    return pl.pallas_call(
        _diff_kernel,
        grid=(batch, DIFF_HEADS, nq),
        in_specs=[pl.BlockSpec((2 * LANE, tq), lambda b, h, i: (h, b * nq + i)),
                  pl.BlockSpec((seq, 2 * LANE), lambda b, h, i: (b, h)),
                  pl.BlockSpec((seq, DIFF_VDIM), lambda b, h, i: (b, h)),
                  pl.BlockSpec((4, DIFF_DH), lambda b, h, i: (0, 0)),
                  pl.BlockSpec((DIFF_VDIM, 1), lambda b, h, i: (0, 0))],
        out_specs=pl.BlockSpec((tq, DIFF_VDIM), lambda b, h, i: (b * nq + i, h)),
        out_shape=jax.ShapeDtypeStruct((batch * seq, DIFF_WIDTH), BF),
        compiler_params=pltpu.CompilerParams(dimension_semantics=("arbitrary",) * 3,
                                             vmem_limit_bytes=VMEM_LIMIT),
        name="diff",
    )(qd_t, kd, vd, lamv, ng_col)


def _merge_kernel(on_ref, od_ref, ga_ref, gb_ref, x_ref, wn_ref, wd_ref, wo_ref, g_ref, b_ref,
                  h_ref, hb_ref):
    merged = (ga_ref[...].astype(F32) * _dot(on_ref[...], wn_ref[...])
              + gb_ref[...].astype(F32) * _dot(od_ref[...], wd_ref[...]))
    mix = _dot(merged.astype(BF), wo_ref[...])
    h = _layer_norm(DEEPNORM_ALPHA * x_ref[...] + mix, g_ref[...], b_ref[...])
    h_ref[...] = h
    hb_ref[...] = h.T.astype(BF)


def _merge(o_nsa, o_diff, ga, gb, x2, wn, wd, wo, g, b, tm=512):
    n = x2.shape[0]
    tok = lambda w: pl.BlockSpec((tm, w), lambda i: (i, 0))
    full = lambda a: pl.BlockSpec(a.shape, lambda i: (0, 0))
    return pl.pallas_call(
        _merge_kernel,
        grid=(n // tm,),
        in_specs=[tok(NSA_WIDTH), tok(DIFF_WIDTH), tok(D_MODEL), tok(D_MODEL), tok(D_MODEL),
                  full(wn), full(wd), full(wo), full(g), full(b)],
        out_specs=[tok(D_MODEL), pl.BlockSpec((D_MODEL, tm), lambda i: (0, i))],
        out_shape=[jax.ShapeDtypeStruct((n, D_MODEL), F32), jax.ShapeDtypeStruct((D_MODEL, n), BF)],
        compiler_params=pltpu.CompilerParams(dimension_semantics=("arbitrary",),
                                             vmem_limit_bytes=VMEM_LIMIT),
        name="merge",
    )(o_nsa, o_diff, ga, gb, x2, wn, wd, wo, g, b)


N_CAND = PEER_TOPK + 8 * (PEER_TOPK - 1)


def _extract_topk(s, k):
    nrow, w = s.shape
    rows = lax.broadcasted_iota(I32, (nrow, w), 0)
    krow = lax.broadcasted_iota(I32, (k, w), 0)
    pos = jnp.full((nrow, w), float(k), F32)
    vals = jnp.zeros((k, w), F32)
    for r in range(k):
        m = jnp.max(s, axis=0, keepdims=True)
        idx = jnp.min(jnp.where(s == m, rows, nrow), axis=0, keepdims=True)
        hit = rows == idx
        pos = jnp.where(hit, float(r), pos)
        s = jnp.where(hit, -jnp.inf, s)
        vals = jnp.where(krow == r, m, vals)
    return vals, pos


def _sort_pairs(n):
    pairs = []

    def merge(lo, hi, r):
        step = r * 2
        if step < hi - lo:
            merge(lo, hi, step)
            merge(lo + r, hi, step)
            pairs.extend((i, i + r) for i in range(lo + r, hi - r, step))
        else:
            pairs.append((lo, lo + r))

    def sort(lo, hi):
        if hi - lo >= 1:
            mid = lo + (hi - lo) // 2
            sort(lo, mid)
            sort(mid + 1, hi)
            merge(lo, hi, 1)

    sort(0, n - 1)
    return pairs


def _sorted_top16(s):
    n = PEER_TOPK
    v = [s[SUBLANES * i:SUBLANES * (i + 1)] for i in range(n)]
    for i, j in _sort_pairs(n):
        v[i], v[j] = jnp.maximum(v[i], v[j]), jnp.minimum(v[i], v[j])
    for shift in (4, 2, 1):
        other = [pltpu.roll(x, shift, 0) for x in v]
        t = [jnp.maximum(v[i], other[n - 1 - i]) for i in range(n)]
        d = n // 2
        while d:
            for i in range(n):
                if not i & d:
                    t[i], t[i + d] = jnp.maximum(t[i], t[i + d]), jnp.minimum(t[i], t[i + d])
            d //= 2
        v = t
    return v


def _route_kernel(h_ref, wq_ref, sk1_ref, sk2_ref, c_ref, p2_ref, e1_ref, e2_ref, q_scr):
    t = h_ref.shape[1]
    q_scr[...] = _dot(wq_ref[...], h_ref[...]).astype(BF)
    half = PEER_DKEY // 2

    crow = lax.broadcasted_iota(I32, (N_CAND, LANE), 0)
    ci = jnp.where(crow < PEER_TOPK, 0, ((crow - PEER_TOPK) >> 3) + 1)
    cj = jnp.where(crow < PEER_TOPK, crow, (crow - PEER_TOPK) & 7)
    cand_ok = (ci + 1) * (cj + 1) <= PEER_TOPK
    krow = lax.broadcasted_iota(I32, (PEER_TOPK, LANE), 0)
    sub = lax.broadcasted_iota(I32, (SUBLANES, LANE), 0)

    def tables(s1, s2):
        v1, pos1 = _extract_topk(s1, PEER_TOPK)
        v2, pos2 = _extract_topk(s2, PEER_TOPK)
        pieces = [v1[0:1] + v2]
        for r in range(1, PEER_TOPK):
            pieces.append(v1[r:r + 1] + v2[0:8])
        cand = jnp.where(cand_ok, jnp.concatenate(pieces, axis=0), -jnp.inf)
        _, cpos = _extract_topk(cand, PEER_TOPK)
        picked = cpos < PEER_TOPK
        top = v1[0:1] + v2[0:1]
        z = jnp.sum(jnp.where(picked, jnp.exp(cand - top), 0.0), axis=0, keepdims=True)
        pk = picked.astype(F32)
        cnt = jnp.zeros((PEER_TOPK, LANE), F32)
        cnt = jnp.where(krow == 0, jnp.sum(pk[0:PEER_TOPK], axis=0, keepdims=True), cnt)
        for r in range(1, PEER_TOPK):
            lo = PEER_TOPK + 8 * (r - 1)
            cnt = jnp.where(krow == r, jnp.sum(pk[lo:lo + 8], axis=0, keepdims=True), cnt)
        c = jnp.zeros((PEER_NKEYS, LANE), F32)
        for r in range(PEER_TOPK):
            c = jnp.where(pos1 == float(r), cnt[r:r + 1], c)
        return c, pos2, jnp.exp(s1 - v1[0:1]) / z, jnp.exp(s2 - v2[0:1])

    def tables_distinct(s1, s2):
        n = PEER_TOPK
        slabs = lambda s: [s[SUBLANES * i:SUBLANES * (i + 1)] for i in range(PEER_NKEYS // SUBLANES)]
        v1 = _sorted_top16(s1)
        v2 = _sorted_top16(s2)
        col = lambda v, base: functools.reduce(
            lambda acc, k: jnp.where(sub == k, v[base + k], acc), range(1, SUBLANES), v[base])
        v2_lo, v2_hi = col(v2, 0), col(v2, SUBLANES)
        pieces = [v1[0] + v2_lo, v1[0] + v2_hi] + [v1[r] + v2_lo for r in range(1, n)]
        cand = jnp.where(cand_ok, jnp.concatenate(pieces, axis=0), -jnp.inf)
        top = v1[0] + v2[0]
        vals, rest = [], cand
        for _ in range(n):
            m = jnp.max(rest, axis=0, keepdims=True)
            rest = jnp.where(rest == m, -jnp.inf, rest)
            vals.append(m)
        tau = jnp.broadcast_to(vals[n - 1], (SUBLANES, LANE))
        z = functools.reduce(lambda acc, m: acc + jnp.exp(m - top[0:1]), vals[1:], jnp.exp(vals[0] - top[0:1]))
        count = lambda pred: jnp.sum(jnp.where(pred, 1.0, 0.0), axis=0, keepdims=True)
        ok = count(cand >= tau[0:1]) == n
        for k in range(n - 1):
            ok = ok & (vals[k] > vals[k + 1]) & (v1[k][0:1] > v1[k + 1][0:1]) & (v2[k][0:1] > v2[k + 1][0:1])
        ok = ok & (count(s1 >= v1[n - 1][0:1]) == n) & (count(s2 >= v2[n - 1][0:1]) == n)
        v1_rows = jnp.concatenate([v1[0], v1[0]] + v1[1:], axis=0)
        kept = jnp.where(cand >= tau[0:1], v1_rows, jnp.inf)
        th_lo = functools.reduce(jnp.minimum, [kept[SUBLANES * i:SUBLANES * (i + 1)] for i in range(2, n + 1)],
                                 kept[0:SUBLANES])
        th_hi = kept[SUBLANES:2 * SUBLANES]
        theta = ([jnp.broadcast_to(th_lo[j:j + 1], (SUBLANES, LANE)) for j in range(SUBLANES)]
                 + [jnp.broadcast_to(th_hi[j:j + 1], (SUBLANES, LANE)) for j in range(SUBLANES)])
        c, p2 = [], []
        for x1, x2 in zip(slabs(s1), slabs(s2)):
            c.append(functools.reduce(lambda acc, k: acc + jnp.where(x1 >= theta[k], 1.0, 0.0),
                                      range(n), jnp.zeros_like(x1)))
            p2.append(functools.reduce(lambda acc, k: acc + jnp.where(v2[k] > x2, 1.0, 0.0),
                                       range(n), jnp.zeros_like(x2)))
        e1 = jnp.exp(s1 - v1[0][0:1]) / z
        e2 = jnp.exp(s2 - v2[0][0:1])
        return (jnp.concatenate(c, axis=0), jnp.concatenate(p2, axis=0), e1, e2), ok

    def head_body(hd, _):
        base = pl.multiple_of(hd * PEER_DKEY, PEER_DKEY)
        s1_all = _dot(sk1_ref[...], q_scr[pl.ds(base, half), :])
        s2_all = _dot(sk2_ref[...], q_scr[pl.ds(base + half, half), :])
        for lc in range(t // LANE):
            s1 = s1_all[:, lc * LANE:(lc + 1) * LANE]
            s2 = s2_all[:, lc * LANE:(lc + 1) * LANE]

            def store(vals):
                for ref, v in zip((c_ref, p2_ref, e1_ref, e2_ref), vals):
                    ref[hd, lc] = v

            vals, distinct = tables_distinct(s1, s2)
            store(vals)

            @pl.when(jnp.min(jnp.where(distinct, 1.0, 0.0)) < 0.5)
            def _():
                store(tables(s1, s2))
        return 0

    lax.fori_loop(0, PEER_HEADS, head_body, 0)


def _route(hb_t, wq_t, sk1, sk2, tm=256):
    n = hb_t.shape[1]
    full = lambda a: pl.BlockSpec(a.shape, lambda i: (0, 0))
    out_spec = pl.BlockSpec((PEER_HEADS, tm // LANE, PEER_NKEYS, LANE), lambda i: (0, i, 0, 0))
    shape = (PEER_HEADS, n // LANE, PEER_NKEYS, LANE)
    return pl.pallas_call(
        _route_kernel,
        grid=(n // tm,),
        in_specs=[pl.BlockSpec((D_MODEL, tm), lambda i: (0, i)), full(wq_t), full(sk1), full(sk2)],
        out_specs=[out_spec] * 4,
        out_shape=[jax.ShapeDtypeStruct(shape, F32)] * 4,
        scratch_shapes=[pltpu.VMEM((PEER_HEADS * PEER_DKEY, tm), BF)],
        compiler_params=pltpu.CompilerParams(dimension_semantics=("arbitrary",),
                                             vmem_limit_bytes=VMEM_LIMIT),
        name="route",
    )(hb_t, wq_t, sk1, sk2)


def _peer_kernel(x_ref, u_ref, vt_ref, c_ref, p2_ref, e1_ref, e2_ref, o_ref,
                 act0, act1, y0, y1, p2_scr, e2_scr, *, n_chunks):
    s = pl.program_id(1)
    te, tm = y0.shape
    n_a = te // PEER_NKEYS
    groups = PEER_NKEYS // BF_TILE[0]

    @pl.when(s == 0)
    def _():
        o_ref[...] = jnp.zeros_like(o_ref)
        for r in (act0, act1, y0, y1):
            r[...] = jnp.zeros_like(r)
        for hd in range(PEER_HEADS):
            for lc in range(tm // LANE):
                p2_scr[hd, lc] = p2_ref[hd, lc].astype(BF).reshape(groups, *BF_TILE)
                e2_scr[hd, lc] = e2_ref[hd, lc].astype(BF).reshape(groups, *BF_TILE)

    jb = jnp.clip(s - 1, 0, n_chunks - 1)

    def step(act_new, act_old, y_new, y_old):
        o_ref[...] += _dot(vt_ref[0], y_old[...])
        for al in range(n_a):
            a = jb * n_a + al
            for lc in range(tm // LANE):
                w = None
                for hd in range(PEER_HEADS):
                    ca = jnp.broadcast_to(c_ref[hd, lc, pl.ds(a, 1), :], BF_TILE).astype(BF)[None]
                    e1a = jnp.broadcast_to(e1_ref[hd, lc, pl.ds(a, 1), :], BF_TILE).astype(BF)[None]
                    term = jnp.where(p2_scr[hd, lc] < ca, e2_scr[hd, lc], 0.0) * e1a
                    w = term if w is None else w + term
                rows = slice(al * PEER_NKEYS, (al + 1) * PEER_NKEYS)
                cols = slice(lc * LANE, (lc + 1) * LANE)
                y = w * act_old[rows, cols].reshape(groups, *BF_TILE)
                y_new[rows, cols] = y.reshape(PEER_NKEYS, LANE)
        act_new[...] = _gelu(_dot(u_ref[...], x_ref[...])).astype(BF)

    @pl.when(s % 2 == 0)
    def _():
        step(act0, act1, y1, y0)

    @pl.when(s % 2 == 1)
    def _():
        step(act1, act0, y0, y1)


def _peer(hb_t, u_b, vt_b, c, p2, e1, e2, tm=1024):
    n = hb_t.shape[1]
    n_chunks, _, te = vt_b.shape
    last = n_chunks - 1
    rt = pl.BlockSpec((PEER_HEADS, tm // LANE, PEER_NKEYS, LANE), lambda i, s: (0, i, 0, 0),
                      pipeline_mode=pl.Buffered(1))
    packed = pltpu.VMEM((PEER_HEADS, tm // LANE, PEER_NKEYS // BF_TILE[0]) + BF_TILE, BF)
    chunk = pltpu.VMEM((te, tm), BF)
    return pl.pallas_call(
        functools.partial(_peer_kernel, n_chunks=n_chunks),
        grid=(n // tm, n_chunks + 2),
        in_specs=[pl.BlockSpec((D_MODEL, tm), lambda i, s: (0, i)),
                  pl.BlockSpec((te, D_MODEL), lambda i, s: (jnp.minimum(s, last), 0)),
                  pl.BlockSpec((1, D_MODEL, te), lambda i, s: (jnp.clip(s - 2, 0, last), 0, 0)),
                  rt, rt, rt, rt],
        out_specs=pl.BlockSpec((D_MODEL, tm), lambda i, s: (0, i)),
        out_shape=jax.ShapeDtypeStruct((D_MODEL, n), F32),
        scratch_shapes=[chunk, chunk, chunk, chunk, packed, packed],
        compiler_params=pltpu.CompilerParams(dimension_semantics=("arbitrary", "arbitrary"),
                                             vmem_limit_bytes=VMEM_LIMIT),
        name="peer",
    )(hb_t, u_b, vt_b, c, p2, e1, e2)


def _final_kernel(ft_ref, h_ref, p_ref, wg_ref, wp_ref, g_ref, b_ref, o_ref):
    ffn = ft_ref[...].T
    h = _layer_norm(DEEPNORM_ALPHA * h_ref[...] + ffn, g_ref[...], b_ref[...])
    gate = jax.nn.sigmoid(_dot(h.astype(BF), wg_ref[...]))
    o_ref[...] = h + gate * _dot(p_ref[...].astype(BF), wp_ref[...])


def _final(ffn_t, h1, p2d, wg, wp, g, b, tm=512):
    n = h1.shape[0]
    full = lambda a: pl.BlockSpec(a.shape, lambda i: (0, 0))
    return pl.pallas_call(
        _final_kernel,
        grid=(n // tm,),
        in_specs=[pl.BlockSpec((D_MODEL, tm), lambda i: (0, i)),
                  pl.BlockSpec((tm, D_MODEL), lambda i: (i, 0)),
                  pl.BlockSpec((tm, PLE_DIM), lambda i: (i, 0)),
                  full(wg), full(wp), full(g), full(b)],
        out_specs=pl.BlockSpec((tm, D_MODEL), lambda i: (i, 0)),
        out_shape=jax.ShapeDtypeStruct((n, D_MODEL), F32),
        compiler_params=pltpu.CompilerParams(dimension_semantics=("arbitrary",),
                                             vmem_limit_bytes=VMEM_LIMIT),
        name="final",
    )(ffn_t, h1, p2d, wg, wp, g, b)


def _arrange_w_in(w_in):
    cuts = [NSA_WIDTH, 6 * NSA_KV, 3 * NSA_HEADS, 2 * DIFF_QK, 2 * DIFF_QK, DIFF_WIDTH, D_MODEL, D_MODEL]
    offs = [0]
    for c in cuts:
        offs.append(offs[-1] + c)
    w_qn, w_kv, w_g, w_qd, w_kd, w_vd, w_ga, w_gb = [w_in[:, offs[k]:offs[k + 1]] for k in range(8)]
    d = w_in.shape[0]

    def widen(w):
        w = w.reshape(d, -1, HEAD_DIM)
        return jnp.pad(w, ((0, 0), (0, 0), (0, LANE - HEAD_DIM))).reshape(d, -1)

    kv = w_kv.reshape(d, 6, NSA_GROUPS, HEAD_DIM)
    both = lambda a, b: jnp.stack([kv[:, a], kv[:, b]], axis=2).reshape(d, NSA_GROUPS * LANE)
    w_g = w_g.reshape(d, NSA_GROUPS, 3 * NSA_HPG)
    w_g = jnp.pad(w_g, ((0, 0), (0, 0), (0, GATE_ROWS - 3 * NSA_HPG))).reshape(d, NSA_GROUPS * GATE_ROWS)
    pair = lambda w: w.reshape(d, 2, DIFF_HEADS, DIFF_DH).transpose(0, 2, 1, 3).reshape(d, 2 * DIFF_QK)
    cols = {"ksel": widen(kv[:, 2].reshape(d, NSA_KV)), "kwin": widen(kv[:, 4].reshape(d, NSA_KV)),
            "vsw": both(3, 5), "kvc": both(0, 1), "kd": widen(pair(w_kd)), "vd": w_vd, "ga": w_ga, "gb": w_gb}
    rows = {"qn": widen(w_qn * (HEAD_DIM ** -0.5)), "g": w_g,
            "qd": widen(pair(w_qd) * (DIFF_DH ** -0.5))}
    w_all = jnp.concatenate([cols[name] for name, _, _ in PROJ_COLS], axis=1).astype(BF)
    wt_all = jnp.concatenate([rows[name] for name, _, _ in PROJ_ROWS], axis=1).T.astype(BF)
    return w_all, wt_all


def _query_aug():
    slot = jnp.arange(LANE)
    on = ((slot == AUG) | (slot == AUG + 1)).astype(F32)
    nsa = jnp.exp2(-(jnp.arange(NSA_HEADS) + 1.0))
    dif = jnp.repeat(jnp.exp2(-(8.0 / DIFF_HEADS) * (jnp.arange(DIFF_HEADS) + 1.0)), 2)
    parts = {"qn": (nsa[:, None] * on).reshape(-1), "g": jnp.zeros((NSA_GROUPS * GATE_ROWS,), F32),
             "qd": (dif[:, None] * on).reshape(-1)}
    return jnp.concatenate([parts[name] for name, _, _ in PROJ_ROWS]).reshape(PROJ_HEIGHT, 1)


def _sel_weights_t():
    c0 = jnp.arange(N_CMP)[None, :] * CMP_STRIDE
    s0 = jnp.arange(LANE)[:, None] * SEL_BLOCK
    ov = jnp.minimum(c0 + CMP_BLOCK, s0 + SEL_BLOCK) - jnp.maximum(c0, s0)
    w = jnp.clip(ov, 0, None).astype(F32) / CMP_BLOCK
    return jnp.where(jnp.arange(LANE)[:, None] < N_SEL, w, 0.0).astype(BF)


def _token_mixers(x2, batch, seq, w_in, cmp_pos_k, cmp_pos_v, cmp_k_w1, cmp_k_w2, cmp_v_w1, cmp_v_w2,
                  lam_q1, lam_k1, lam_q2, lam_k2, diff_norm_g):
    w_all, wt_all = _arrange_w_in(w_in)
    ksel, kwin, vsw, kvc, kd, vd, ga, gb, qn_t, gates_t, qd_t = _proj(x2, w_all, wt_all, _query_aug(), seq)
    r = kvc.reshape(batch, seq, NSA_GROUPS, 2, HEAD_DIM).transpose(0, 2, 3, 1, 4)
    r = r.reshape(batch, NSA_GROUPS, 2, seq // CMP_STRIDE, CMP_STRIDE * HEAD_DIM)
    pos2 = lambda pe: pe.reshape(2, CMP_STRIDE * HEAD_DIM)
    cmp = _compress(r, pos2(cmp_pos_k), pos2(cmp_pos_v), cmp_k_w1.astype(BF), cmp_k_w2.astype(BF),
                    cmp_v_w1.astype(BF), cmp_v_w2.astype(BF))
    o_nsa = _nsa(qn_t, ksel, kwin, vsw, cmp, gates_t, _sel_weights_t(), batch, seq)
    lamv = jnp.stack([lam_q1, lam_k1, lam_q2, lam_k2]).astype(F32)
    o_diff = _diff(qd_t, kd, vd, lamv, diff_norm_g.reshape(DIFF_VDIM, 1).astype(F32), batch, seq)
    return o_nsa, o_diff, ga, gb


def kernel(x, p, w_in, cmp_pos_k, cmp_pos_v, cmp_k_w1, cmp_k_w2, cmp_v_w1, cmp_v_w2, lam_q1, lam_k1, lam_q2, lam_k2, diff_norm_g, w_branch_nsa, w_branch_diff, w_out, ln1_g, ln1_b, peer_wq, peer_subkeys1, peer_subkeys2, peer_u, peer_v, ln2_g, ln2_b, ple_w_proj, ple_w_gate):
    batch, seq, d = x.shape
    assert seq == N_CMP * CMP_STRIDE and seq == N_SEL * SEL_BLOCK and d == D_MODEL
    x2 = x.reshape(batch * seq, d)
    row = lambda v: v.reshape(1, -1).astype(F32)
    o_nsa, o_diff, ga, gb = _token_mixers(
        x2, batch, seq, w_in[0], cmp_pos_k[0], cmp_pos_v[0], cmp_k_w1[0], cmp_k_w2[0], cmp_v_w1[0],
        cmp_v_w2[0], lam_q1[0], lam_k1[0], lam_q2[0], lam_k2[0], diff_norm_g[0])
    h1, h1b = _merge(o_nsa, o_diff, ga, gb, x2, w_branch_nsa[0].astype(BF), w_branch_diff[0].astype(BF),
                     w_out[0].astype(BF), row(ln1_g[0]), row(ln1_b[0]))
    c, p2, e1, e2 = _route(h1b, peer_wq[0].T.astype(BF), peer_subkeys1[0].astype(BF),
                           peer_subkeys2[0].astype(BF))
    vt = peer_v[0].astype(BF).reshape(PEER_NEXPERTS // PEER_CHUNK, PEER_CHUNK, d).transpose(0, 2, 1)
    ffn_t = _peer(h1b, peer_u[0].astype(BF), vt, c, p2, e1, e2)
    out = _final(ffn_t, h1, p[0].reshape(batch * seq, PLE_DIM), ple_w_gate[0].astype(BF),
                 ple_w_proj[0].astype(BF), row(ln2_g[0]), row(ln2_b[0]))
    return out.reshape(batch, seq, d)
```

```python
import functools
import math

import jax
import jax.numpy as jnp
from jax import lax
from jax.experimental import pallas as pl
from jax.experimental.pallas import tpu as pltpu

D_MODEL = 1024
DEPTH = 1
PLE_DIM = 256

NSA_HEADS = 8
NSA_GROUPS = 2
NSA_HPG = NSA_HEADS // NSA_GROUPS
HEAD_DIM = 64
CMP_BLOCK = 32
CMP_STRIDE = 16
SEL_BLOCK = 64
SEL_TOPK = 8
WINDOW = 512
SEL_FORCE = 1.0e4

DIFF_HEADS = 4
DIFF_DH = 64
DIFF_VDIM = 2 * DIFF_DH

PEER_HEADS = 8
PEER_NKEYS = 128
PEER_NEXPERTS = PEER_NKEYS * PEER_NKEYS
PEER_DKEY = 256
PEER_TOPK = 16

NSA_WIDTH = NSA_HEADS * HEAD_DIM
NSA_KV = NSA_GROUPS * HEAD_DIM
DIFF_QK = DIFF_HEADS * DIFF_DH
DIFF_WIDTH = DIFF_HEADS * DIFF_VDIM

DEEPNORM_ALPHA = (2.0 * DEPTH) ** 0.25
LN_EPS = 1e-5

BF = jnp.bfloat16
F32 = jnp.float32
I32 = jnp.int32

LANE = 128
SUBLANES = 8
BF_TILE = (16, LANE)
Q_TILE_NSA = 512
Q_TILE_DIFF = 1024
N_CMP = 128
N_SEL = 32
PEER_CHUNK = 512
VMEM_LIMIT = 56 * 1024 * 1024

AUG = HEAD_DIM
SEL_LANE0 = AUG + 8
M_INIT = -1.0e30
MASKED = -(2.0 ** 100)
GATE_ROWS = 16

PROJ_COLS = (("ksel", NSA_GROUPS * LANE, BF), ("kwin", NSA_GROUPS * LANE, BF), ("vsw", NSA_GROUPS * LANE, BF),
             ("kvc", NSA_GROUPS * LANE, BF), ("kd", 2 * DIFF_HEADS * LANE, BF), ("vd", DIFF_WIDTH, BF),
             ("ga", D_MODEL, BF), ("gb", D_MODEL, BF))
PROJ_WIDTH = sum(w for _, w, _ in PROJ_COLS)
PROJ_ROWS = (("qn", NSA_HEADS * LANE, BF), ("g", NSA_GROUPS * GATE_ROWS, F32), ("qd", 2 * DIFF_HEADS * LANE, BF))
PROJ_HEIGHT = sum(r for _, r, _ in PROJ_ROWS)


def _dot(a, b):
    return jnp.dot(a, b, preferred_element_type=F32)


def _dot_nt(a, b):
    return lax.dot_general(a, b, (((1,), (1,)), ((), ())), preferred_element_type=F32)


def _dot_tn(a, b):
    return lax.dot_general(a, b, (((0,), (0,)), ((), ())), preferred_element_type=F32)


def _gelu(x):
    return 0.5 * x * (1.0 + lax.erf(x * (2.0 ** -0.5)))


def _layer_norm(y, g, b):
    mu = jnp.mean(y, axis=-1, keepdims=True)
    d = y - mu
    var = jnp.mean(d * d, axis=-1, keepdims=True)
    return d * lax.rsqrt(var + LN_EPS) * g + b


def _split_bf16(v):
    hi = v.astype(BF).astype(F32)
    return hi, v - hi


def _proj_kernel(x_ref, w_ref, wt_ref, qaug_ref, *o_refs, seq):
    i = pl.program_id(0)
    tm = x_ref.shape[0]
    x = x_ref[...].astype(BF)
    t = (i * tm + lax.broadcasted_iota(I32, (tm, LANE), 0)) & (seq - 1)
    lane = lax.broadcasted_iota(I32, (tm, LANE), 1)
    hi, lo = _split_bf16(t.astype(F32))
    k_aug = jnp.where(lane == AUG, hi, jnp.where(lane == AUG + 1, lo, 0.0))
    k_aug_sel = k_aug + jnp.where(lane - SEL_LANE0 == (t >> 6), 1.0, 0.0)
    extra = {"ksel": lambda: jnp.concatenate([k_aug_sel] * NSA_GROUPS, axis=1),
             "kwin": lambda: jnp.concatenate([k_aug] * NSA_GROUPS, axis=1),
             "kd": lambda: jnp.concatenate([k_aug] * (2 * DIFF_HEADS), axis=1)}
    c0 = 0
    for (name, width, dt), o_ref in zip(PROJ_COLS, o_refs):
        v = _dot(x, w_ref[:, c0:c0 + width])
        c0 += width
        if name in extra:
            v = v + extra[name]()
        if name in ("ga", "gb"):
            v = jax.nn.sigmoid(v)
        o_ref[...] = v.astype(dt)
    r0 = 0
    for (name, rows, dt), o_ref in zip(PROJ_ROWS, o_refs[len(PROJ_COLS):]):
        v = _dot_nt(wt_ref[r0:r0 + rows, :], x) + qaug_ref[r0:r0 + rows, :]
        r0 += rows
        if name == "g":
            v = jax.nn.sigmoid(v)
        o_ref[...] = v.astype(dt)


def _proj(x2, w_all, wt_all, qaug, seq, tm=256):
    n = x2.shape[0]
    assert seq & (seq - 1) == 0 and seq % tm == 0
    full = lambda a: pl.BlockSpec(a.shape, lambda i: (0, 0))
    return pl.pallas_call(
        functools.partial(_proj_kernel, seq=seq),
        grid=(n // tm,),
        in_specs=[pl.BlockSpec((tm, D_MODEL), lambda i: (i, 0)), full(w_all), full(wt_all), full(qaug)],
        out_specs=([pl.BlockSpec((tm, w), lambda i: (i, 0)) for _, w, _ in PROJ_COLS]
                   + [pl.BlockSpec((r, tm), lambda i: (0, i)) for _, r, _ in PROJ_ROWS]),
        out_shape=([jax.ShapeDtypeStruct((n, w), dt) for _, w, dt in PROJ_COLS]
                   + [jax.ShapeDtypeStruct((r, n), dt) for _, r, dt in PROJ_ROWS]),
        compiler_params=pltpu.CompilerParams(dimension_semantics=("arbitrary",),
                                             vmem_limit_bytes=VMEM_LIMIT),
        name="proj",
    )(x2, w_all, wt_all, qaug)


def _compress_kernel(r_ref, posk_ref, posv_ref, wk1_ref, wk2_ref, wv1_ref, wv2_ref, o_ref):
    half = CMP_STRIDE * HEAD_DIM
    for idx, (pos_ref, w1_ref, w2_ref) in enumerate(((posk_ref, wk1_ref, wk2_ref),
                                                     (posv_ref, wv1_ref, wv2_ref))):
        r = r_ref[0, 0, idx].astype(F32)
        ra = (r + pos_ref[0:1, :]).astype(BF)
        rb = (r + pos_ref[1:2, :]).astype(BF)
        p1 = _dot(ra, w1_ref[0:half, :])
        p2 = _dot(rb, w1_ref[half:2 * half, :])
        hid = p1 + pltpu.roll(p2, N_CMP - 1, 0)
        c = _dot(_gelu(hid).astype(BF), w2_ref[...])
        o_ref[0, 0, :, idx * LANE:idx * LANE + HEAD_DIM] = c.astype(BF)
    lane = lax.broadcasted_iota(I32, (N_CMP, HEAD_DIM), 1)
    c_end = lax.broadcasted_iota(I32, (N_CMP, HEAD_DIM), 0) * CMP_STRIDE + (CMP_BLOCK - 1)
    hi, lo = _split_bf16(c_end.astype(F32))
    o_ref[0, 0, :, AUG:LANE] = jnp.where(lane == 0, hi, jnp.where(lane == 1, lo, 0.0)).astype(BF)
    o_ref[0, 0, :, LANE + HEAD_DIM:2 * LANE] = jnp.zeros((N_CMP, HEAD_DIM), BF)


def _compress(r, posk, posv, wk1, wk2, wv1, wv2):
    b = r.shape[0]
    full = lambda a: pl.BlockSpec(a.shape, lambda i, j: (0,) * a.ndim)
    return pl.pallas_call(
        _compress_kernel,
        grid=(b, NSA_GROUPS),
        in_specs=[pl.BlockSpec((1, 1, 2, N_CMP, CMP_STRIDE * HEAD_DIM), lambda i, j: (i, j, 0, 0, 0)),
                  full(posk), full(posv), full(wk1), full(wk2), full(wv1), full(wv2)],
        out_specs=pl.BlockSpec((1, 1, N_CMP, 2 * LANE), lambda i, j: (i, j, 0, 0)),
        out_shape=jax.ShapeDtypeStruct((b, NSA_GROUPS, N_CMP, 2 * LANE), BF),
        compiler_params=pltpu.CompilerParams(dimension_semantics=("arbitrary", "arbitrary")),
        name="compress",
    )(r, posk, posv, wk1, wk2, wv1, wv2)


def _flash_step(carry, s, v):
    m, l, acc = carry
    m_new = jnp.maximum(m, jnp.max(s, axis=0, keepdims=True))
    a = jnp.exp(m - m_new)
    p = jnp.exp(s - m_new)
    l = a * l + jnp.sum(p, axis=0, keepdims=True)
    acc = a * acc + _dot_tn(v, p.astype(BF))
    return m_new, l, acc


def _flash_init(rows, dv):
    return (jnp.full((1, rows), M_INIT, F32), jnp.zeros((1, rows), F32), jnp.zeros((dv, rows), F32))


def _flash_out(carry):
    _, l, acc = carry
    return acc / jnp.maximum(l, 1e-30)


def _nsa_kernel(qt_ref, ksel_ref, kwin_ref, vsw_ref, cmp_ref, gt_ref, selwt_ref, o_ref):
    i = pl.program_id(2)
    tq = ck = Q_TILE_NSA
    rows = NSA_HPG * tq
    qt = qt_ref[...]
    q4 = jnp.concatenate([qt[h * LANE:(h + 1) * LANE, :] for h in range(NSA_HPG)], axis=1)
    t_lane = i * tq + (lax.broadcasted_iota(I32, (1, rows), 1) & (tq - 1))
    key_col = lax.broadcasted_iota(I32, (ck, 1), 0)
    causal = (i * ck + key_col) <= t_lane

    kc = cmp_ref[0, 0, :, 0:LANE]
    vc = cmp_ref[0, 0, :, LANE:LANE + HEAD_DIM]
    c_end = lax.broadcasted_iota(I32, (N_CMP, 1), 0) * CMP_STRIDE + (CMP_BLOCK - 1)
    s = jnp.where(c_end <= t_lane, _dot(kc, q4), MASKED)
    m = jnp.maximum(jnp.max(s, axis=0, keepdims=True), M_INIT)
    e = jnp.exp(s - m)
    p_cmp = e / jnp.maximum(jnp.sum(e, axis=0, keepdims=True), 1e-30)
    o_cmp = _dot_tn(vc, p_cmp.astype(BF))

    psum = p_cmp[:, 0:tq] + p_cmp[:, tq:2 * tq] + p_cmp[:, 2 * tq:3 * tq] + p_cmp[:, 3 * tq:4 * tq]
    p_hi = psum.astype(BF)
    p_lo = (psum - p_hi.astype(F32)).astype(BF)
    selwt = selwt_ref[...]
    imp = (_dot(selwt, p_hi) + _dot(selwt, p_lo))[0:N_SEL]
    jb = lax.broadcasted_iota(I32, (N_SEL, tq), 0)
    tq_pos = i * tq + lax.broadcasted_iota(I32, (N_SEL, tq), 1)
    cur = tq_pos >> 6
    forced = (jb == 0) | (jb == cur) | (jb == cur - 1)
    score = jnp.where(forced, SEL_FORCE, jnp.where(jb * SEL_BLOCK <= tq_pos, imp, -SEL_FORCE))
    unsel = jnp.full((N_SEL, tq), MASKED, F32)
    for _ in range(SEL_TOPK):
        mx = jnp.max(score, axis=0, keepdims=True)
        idx = jnp.min(jnp.where(score == mx, jb, N_SEL), axis=0, keepdims=True)
        hit = jb == idx
        unsel = jnp.where(hit, 0.0, unsel)
        score = jnp.where(hit, -jnp.inf, score)
    neg = jnp.concatenate([jnp.zeros((SEL_LANE0, tq), F32), unsel,
                           jnp.zeros((LANE - SEL_LANE0 - N_SEL, tq), F32)], axis=0)
    q4s = (q4.astype(F32) + jnp.concatenate([neg] * NSA_HPG, axis=1)).astype(BF)

    def chunk(ref, j, c0, width):
        start = pl.multiple_of(j * ck, ck)
        return ref[pl.ds(start, ck), c0:c0 + width]

    def sel_body(j, carry):
        return _flash_step(carry, _dot(chunk(ksel_ref, j, 0, LANE), q4s), chunk(vsw_ref, j, 0, HEAD_DIM))

    carry = lax.fori_loop(0, i, sel_body, _flash_init(rows, HEAD_DIM))
    s = jnp.where(causal, _dot(chunk(ksel_ref, i, 0, LANE), q4s), MASKED)
    o_slc = _flash_out(_flash_step(carry, s, chunk(vsw_ref, i, 0, HEAD_DIM)))

    carry = _flash_init(rows, HEAD_DIM)
    n_back = WINDOW // ck
    for back in range(n_back, 0, -1):
        j = jnp.maximum(i - back, 0)
        ok = i >= back
        if back == n_back:
            ok = ok & (t_lane - (j * ck + key_col) < WINDOW)
        s = jnp.where(ok, _dot(chunk(kwin_ref, j, 0, LANE), q4), MASKED)
        carry = _flash_step(carry, s, chunk(vsw_ref, j, HEAD_DIM, HEAD_DIM))
    s = jnp.where(causal, _dot(chunk(kwin_ref, i, 0, LANE), q4), MASKED)
    o_win = _flash_out(_flash_step(carry, s, chunk(vsw_ref, i, HEAD_DIM, HEAD_DIM)))

    gates = gt_ref[...]
    outs = []
    for h in range(NSA_HPG):
        c0 = h * tq
        outs.append(gates[3 * h:3 * h + 1] * o_cmp[:, c0:c0 + tq]
                    + gates[3 * h + 1:3 * h + 2] * o_slc[:, c0:c0 + tq]
                    + gates[3 * h + 2:3 * h + 3] * o_win[:, c0:c0 + tq])
    o_ref[...] = jnp.concatenate(outs, axis=0).T.astype(BF)


def _nsa(qn_t, ksel, kwin, vsw, cmp, gates_t, selwt, batch, seq):
    tq = Q_TILE_NSA
    assert WINDOW % tq == 0 and seq % tq == 0
    nq = seq // tq
    per_seq = pl.BlockSpec((seq, LANE), lambda b, g, i: (b, g))
    return pl.pallas_call(
        _nsa_kernel,
        grid=(batch, NSA_GROUPS, nq),
        in_specs=[pl.BlockSpec((NSA_HPG * LANE, tq), lambda b, g, i: (g, b * nq + i)),
                  per_seq, per_seq, per_seq,
                  pl.BlockSpec((1, 1, N_CMP, 2 * LANE), lambda b, g, i: (b, g, 0, 0)),
                  pl.BlockSpec((GATE_ROWS, tq), lambda b, g, i: (g, b * nq + i)),
                  pl.BlockSpec((LANE, LANE), lambda b, g, i: (0, 0))],
        out_specs=pl.BlockSpec((tq, NSA_HPG * HEAD_DIM), lambda b, g, i: (b * nq + i, g)),
        out_shape=jax.ShapeDtypeStruct((batch * seq, NSA_WIDTH), BF),
        compiler_params=pltpu.CompilerParams(dimension_semantics=("arbitrary",) * 3,
                                             vmem_limit_bytes=VMEM_LIMIT),
        name="nsa",
    )(qn_t, ksel, kwin, vsw, cmp, gates_t, selwt)


def _diff_kernel(qt_ref, k_ref, v_ref, lam_ref, ng_ref, o_ref):
    i = pl.program_id(2)
    tq = ck = Q_TILE_DIFF
    q1 = qt_ref[0:LANE, :]
    q2 = qt_ref[LANE:2 * LANE, :]
    lam_init = 0.8 - 0.6 * math.exp(-0.3 * 0)
    lv = lam_ref[...]
    lam = (jnp.exp(jnp.sum(lv[0:1] * lv[1:2], axis=-1, keepdims=True))
           - jnp.exp(jnp.sum(lv[2:3] * lv[3:4], axis=-1, keepdims=True)) + lam_init)
    t_lane = i * tq + lax.broadcasted_iota(I32, (1, tq), 1)
    causal = (i * ck + lax.broadcasted_iota(I32, (ck, 1), 0)) <= t_lane

    def chunk(j):
        start = pl.multiple_of(j * ck, ck)
        return (k_ref[pl.ds(start, ck), 0:LANE], k_ref[pl.ds(start, ck), LANE:2 * LANE],
                v_ref[pl.ds(start, ck), :])

    def body(j, carry):
        c1, c2 = carry
        k1, k2, v = chunk(j)
        return _flash_step(c1, _dot(k1, q1), v), _flash_step(c2, _dot(k2, q2), v)

    init = (_flash_init(tq, DIFF_VDIM), _flash_init(tq, DIFF_VDIM))
    c1, c2 = lax.fori_loop(0, i, body, init)
    k1, k2, v = chunk(i)
    c1 = _flash_step(c1, jnp.where(causal, _dot(k1, q1), MASKED), v)
    c2 = _flash_step(c2, jnp.where(causal, _dot(k2, q2), MASKED), v)
    o = _flash_out(c1) - lam * _flash_out(c2)
    o = o * lax.rsqrt(jnp.mean(o * o, axis=0, keepdims=True) + 1e-5) * ng_ref[...] * (1.0 - lam_init)
    o_ref[...] = o.T.astype(BF)


def _diff(qd_t, kd, vd, lamv, ng_col, batch, seq):
    tq = Q_TILE_DIFF
    nq = seq // tq
    return pl.pallas_call(
        _diff_kernel,
        grid=(batch, DIFF_HEADS, nq),
        in_specs=[pl.BlockSpec((2 * LANE, tq), lambda b, h, i: (h, b * nq + i)),
                  pl.BlockSpec((seq, 2 * LANE), lambda b, h, i: (b, h)),
                  pl.BlockSpec((seq, DIFF_VDIM), lambda b, h, i: (b, h)),
                  pl.BlockSpec((4, DIFF_DH), lambda b, h, i: (0, 0)),
                  pl.BlockSpec((DIFF_VDIM, 1), lambda b, h, i: (0, 0))],
        out_specs=pl.BlockSpec((tq, DIFF_VDIM), lambda b, h, i: (b * nq + i, h)),
        out_shape=jax.ShapeDtypeStruct((batch * seq, DIFF_WIDTH), BF),
        compiler_params=pltpu.CompilerParams(dimension_semantics=("arbitrary",) * 3,
                                             vmem_limit_bytes=VMEM_LIMIT),
        name="diff",
    )(qd_t, kd, vd, lamv, ng_col)


def _merge_kernel(on_ref, od_ref, ga_ref, gb_ref, x_ref, wn_ref, wd_ref, wo_ref, g_ref, b_ref,
                  h_ref, hb_ref):
    merged = (ga_ref[...].astype(F32) * _dot(on_ref[...], wn_ref[...])
              + gb_ref[...].astype(F32) * _dot(od_ref[...], wd_ref[...]))
    mix = _dot(merged.astype(BF), wo_ref[...])
    h = _layer_norm(DEEPNORM_ALPHA * x_ref[...] + mix, g_ref[...], b_ref[...])
    h_ref[...] = h
    hb_ref[...] = h.T.astype(BF)


def _merge(o_nsa, o_diff, ga, gb, x2, wn, wd, wo, g, b, tm=512):
    n = x2.shape[0]
    tok = lambda w: pl.BlockSpec((tm, w), lambda i: (i, 0))
    full = lambda a: pl.BlockSpec(a.shape, lambda i: (0, 0))
    return pl.pallas_call(
        _merge_kernel,
        grid=(n // tm,),
        in_specs=[tok(NSA_WIDTH), tok(DIFF_WIDTH), tok(D_MODEL), tok(D_MODEL), tok(D_MODEL),
                  full(wn), full(wd), full(wo), full(g), full(b)],
        out_specs=[tok(D_MODEL), pl.BlockSpec((D_MODEL, tm), lambda i: (0, i))],
        out_shape=[jax.ShapeDtypeStruct((n, D_MODEL), F32), jax.ShapeDtypeStruct((D_MODEL, n), BF)],
        compiler_params=pltpu.CompilerParams(dimension_semantics=("arbitrary",),
                                             vmem_limit_bytes=VMEM_LIMIT),
        name="merge",
    )(o_nsa, o_diff, ga, gb, x2, wn, wd, wo, g, b)


N_CAND = PEER_TOPK + 8 * (PEER_TOPK - 1)


def _extract_topk(s, k):
    nrow, w = s.shape
    rows = lax.broadcasted_iota(I32, (nrow, w), 0)
    krow = lax.broadcasted_iota(I32, (k, w), 0)
    pos = jnp.full((nrow, w), float(k), F32)
    vals = jnp.zeros((k, w), F32)
    for r in range(k):
        m = jnp.max(s, axis=0, keepdims=True)
        idx = jnp.min(jnp.where(s == m, rows, nrow), axis=0, keepdims=True)
        hit = rows == idx
        pos = jnp.where(hit, float(r), pos)
        s = jnp.where(hit, -jnp.inf, s)
        vals = jnp.where(krow == r, m, vals)
    return vals, pos


def _sort_pairs(n):
    pairs = []

    def merge(lo, hi, r):
        step = r * 2
        if step < hi - lo:
            merge(lo, hi, step)
            merge(lo + r, hi, step)
            pairs.extend((i, i + r) for i in range(lo + r, hi - r, step))
        else:
            pairs.append((lo, lo + r))

    def sort(lo, hi):
        if hi - lo >= 1:
            mid = lo + (hi - lo) // 2
            sort(lo, mid)
            sort(mid + 1, hi)
            merge(lo, hi, 1)

    sort(0, n - 1)
    return pairs


def _sorted_top16(s):
    n = PEER_TOPK
    v = [s[SUBLANES * i:SUBLANES * (i + 1)] for i in range(n)]
    for i, j in _sort_pairs(n):
        v[i], v[j] = jnp.maximum(v[i], v[j]), jnp.minimum(v[i], v[j])
    for shift in (4, 2, 1):
        other = [pltpu.roll(x, shift, 0) for x in v]
        t = [jnp.maximum(v[i], other[n - 1 - i]) for i in range(n)]
        d = n // 2
        while d:
            for i in range(n):
                if not i & d:
                    t[i], t[i + d] = jnp.maximum(t[i], t[i + d]), jnp.minimum(t[i], t[i + d])
            d //= 2
        v = t
    return v


def _route_kernel(h_ref, wq_ref, sk1_ref, sk2_ref, c_ref, p2_ref, e1_ref, e2_ref, q_scr):
    t = h_ref.shape[1]
    q_scr[...] = _dot(wq_ref[...], h_ref[...]).astype(BF)
    half = PEER_DKEY // 2

    crow = lax.broadcasted_iota(I32, (N_CAND, LANE), 0)
    ci = jnp.where(crow < PEER_TOPK, 0, ((crow - PEER_TOPK) >> 3) + 1)
    cj = jnp.where(crow < PEER_TOPK, crow, (crow - PEER_TOPK) & 7)
    cand_ok = (ci + 1) * (cj + 1) <= PEER_TOPK
    krow = lax.broadcasted_iota(I32, (PEER_TOPK, LANE), 0)
    sub = lax.broadcasted_iota(I32, (SUBLANES, LANE), 0)
    frow = lax.broadcasted_iota(I32, (PEER_NKEYS, LANE), 0)
    fi = jnp.maximum((frow >> 3) - 1, 0)
    fj = jnp.where(frow < 2 * SUBLANES, frow, frow & 7)
    cand_fast_ok = ((fi + 1) * (fj + 1) <= PEER_TOPK) | ((frow >> 3 == PEER_TOPK - 1) & (fj == 1))

    def tables(s1, s2):
        v1, pos1 = _extract_topk(s1, PEER_TOPK)
        v2, pos2 = _extract_topk(s2, PEER_TOPK)
        pieces = [v1[0:1] + v2]
        for r in range(1, PEER_TOPK):
            pieces.append(v1[r:r + 1] + v2[0:8])
        cand = jnp.where(cand_ok, jnp.concatenate(pieces, axis=0), -jnp.inf)
        _, cpos = _extract_topk(cand, PEER_TOPK)
        picked = cpos < PEER_TOPK
        top = v1[0:1] + v2[0:1]
        z = jnp.sum(jnp.where(picked, jnp.exp(cand - top), 0.0), axis=0, keepdims=True)
        pk = picked.astype(F32)
        cnt = jnp.zeros((PEER_TOPK, LANE), F32)
        cnt = jnp.where(krow == 0, jnp.sum(pk[0:PEER_TOPK], axis=0, keepdims=True), cnt)
        for r in range(1, PEER_TOPK):
            lo = PEER_TOPK + 8 * (r - 1)
            cnt = jnp.where(krow == r, jnp.sum(pk[lo:lo + 8], axis=0, keepdims=True), cnt)
        c = jnp.zeros((PEER_NKEYS, LANE), F32)
        for r in range(PEER_TOPK):
            c = jnp.where(pos1 == float(r), cnt[r:r + 1], c)
        return c, pos2, jnp.exp(s1 - v1[0:1]) / z, jnp.exp(s2 - v2[0:1])

    def tables_distinct(s1, s2):
        n = PEER_TOPK
        slabs = lambda s: [s[SUBLANES * i:SUBLANES * (i + 1)] for i in range(PEER_NKEYS // SUBLANES)]
        v1 = _sorted_top16(s1)
        v2 = _sorted_top16(s2)
        col = lambda v, base: functools.reduce(
            lambda acc, k: jnp.where(sub == k, v[base + k], acc), range(1, SUBLANES), v[base])
        v2_lo, v2_hi = col(v2, 0), col(v2, SUBLANES)
        last = jnp.where(sub == 1, v1[n - 1] + v2[0], v1[n - 2] + v2_lo)
        pieces = [v1[0] + v2_lo, v1[0] + v2_hi] + [v1[r] + v2_lo for r in range(1, n - 2)] + [last]
        cand = jnp.where(cand_fast_ok, jnp.concatenate(pieces, axis=0), -jnp.inf)
        first = [v1[0], v1[0]] + v1[1:n - 2] + [jnp.where(sub == 1, v1[n - 1], v1[n - 2])]
        top = v1[0] + v2[0]
        vals = _sorted_top16(cand)
        tau = vals[n - 1]
        z = functools.reduce(lambda acc, m: acc + jnp.exp(m[0:1] - top[0:1]), vals[1:],
                             jnp.exp(vals[0][0:1] - top[0:1]))
        count = lambda pred: jnp.sum(jnp.where(pred, 1.0, 0.0), axis=0, keepdims=True)
        ok = count(cand >= tau[0:1]) == n
        for k in range(n - 1):
            ok = (ok & (vals[k][0:1] > vals[k + 1][0:1]) & (v1[k][0:1] > v1[k + 1][0:1])
                  & (v2[k][0:1] > v2[k + 1][0:1]))
        ok = ok & (count(s1 >= v1[n - 1][0:1]) == n) & (count(s2 >= v2[n - 1][0:1]) == n)
        kept = [jnp.where(p >= tau, f, jnp.inf) for p, f in zip(jnp.split(cand, n, axis=0), first)]
        odd = jnp.min(jnp.where(sub == 1, kept[n - 1], jnp.inf), axis=0, keepdims=True)
        kept[n - 1] = jnp.where(sub == 1, jnp.inf, kept[n - 1])
        th_lo = functools.reduce(jnp.minimum, kept[2:], kept[0])
        th_lo = jnp.where(sub == 0, jnp.minimum(th_lo, odd), th_lo)
        th_hi = kept[1]
        theta = ([jnp.broadcast_to(th_lo[j:j + 1], (SUBLANES, LANE)) for j in range(SUBLANES)]
                 + [jnp.broadcast_to(th_hi[j:j + 1], (SUBLANES, LANE)) for j in range(SUBLANES)])
        c, p2 = [], []
        for x1, x2 in zip(slabs(s1), slabs(s2)):
            c.append(functools.reduce(lambda acc, k: acc + jnp.where(x1 >= theta[k], 1.0, 0.0),
                                      range(n), jnp.zeros_like(x1)))
            p2.append(functools.reduce(lambda acc, k: acc + jnp.where(v2[k] > x2, 1.0, 0.0),
                                       range(n), jnp.zeros_like(x2)))
        e1 = jnp.exp(s1 - v1[0][0:1]) / z
        e2 = jnp.exp(s2 - v2[0][0:1])
        return (jnp.concatenate(c, axis=0), jnp.concatenate(p2, axis=0), e1, e2), ok

    def head_body(hd, _):
        base = pl.multiple_of(hd * PEER_DKEY, PEER_DKEY)
        s1_all = _dot(sk1_ref[...], q_scr[pl.ds(base, half), :])
        s2_all = _dot(sk2_ref[...], q_scr[pl.ds(base + half, half), :])
        for lc in range(t // LANE):
            s1 = s1_all[:, lc * LANE:(lc + 1) * LANE]
            s2 = s2_all[:, lc * LANE:(lc + 1) * LANE]

            def store(vals):
                for ref, v in zip((c_ref, p2_ref, e1_ref, e2_ref), vals):
                    ref[hd, lc] = v

            vals, distinct = tables_distinct(s1, s2)
            store(vals)

            @pl.when(jnp.min(jnp.where(distinct, 1.0, 0.0)) < 0.5)
            def _():
                store(tables(s1, s2))
        return 0

    lax.fori_loop(0, PEER_HEADS, head_body, 0)


def _route(hb_t, wq_t, sk1, sk2, tm=256):
    n = hb_t.shape[1]
    full = lambda a: pl.BlockSpec(a.shape, lambda i: (0, 0))
    out_spec = pl.BlockSpec((PEER_HEADS, tm // LANE, PEER_NKEYS, LANE), lambda i: (0, i, 0, 0))
    shape = (PEER_HEADS, n // LANE, PEER_NKEYS, LANE)
    return pl.pallas_call(
        _route_kernel,
        grid=(n // tm,),
        in_specs=[pl.BlockSpec((D_MODEL, tm), lambda i: (0, i)), full(wq_t), full(sk1), full(sk2)],
        out_specs=[out_spec] * 4,
        out_shape=[jax.ShapeDtypeStruct(shape, F32)] * 4,
        scratch_shapes=[pltpu.VMEM((PEER_HEADS * PEER_DKEY, tm), BF)],
        compiler_params=pltpu.CompilerParams(dimension_semantics=("arbitrary",),
                                             vmem_limit_bytes=VMEM_LIMIT),
        name="route",
    )(hb_t, wq_t, sk1, sk2)


def _peer_kernel(x_ref, u_ref, vt_ref, c_ref, p2_ref, e1_ref, e2_ref, o_ref,
                 act0, act1, y0, y1, p2_scr, e2_scr, *, n_chunks):
    s = pl.program_id(1)
    te, tm = y0.shape
    n_a = te // PEER_NKEYS
    groups = PEER_NKEYS // BF_TILE[0]

    @pl.when(s == 0)
    def _():
        o_ref[...] = jnp.zeros_like(o_ref)
        for r in (act0, act1, y0, y1):
            r[...] = jnp.zeros_like(r)
        for hd in range(PEER_HEADS):
            for lc in range(tm // LANE):
                p2_scr[hd, lc] = p2_ref[hd, lc].astype(BF).reshape(groups, *BF_TILE)
                e2_scr[hd, lc] = e2_ref[hd, lc].astype(BF).reshape(groups, *BF_TILE)

    jb = jnp.clip(s - 1, 0, n_chunks - 1)

    def step(act_new, act_old, y_new, y_old):
        o_ref[...] += _dot(vt_ref[0], y_old[...])
        for al in range(n_a):
            a = jb * n_a + al
            for lc in range(tm // LANE):
                w = None
                for hd in range(PEER_HEADS):
                    ca = jnp.broadcast_to(c_ref[hd, lc, pl.ds(a, 1), :], BF_TILE).astype(BF)[None]
                    e1a = jnp.broadcast_to(e1_ref[hd, lc, pl.ds(a, 1), :], BF_TILE).astype(BF)[None]
                    term = jnp.where(p2_scr[hd, lc] < ca, e2_scr[hd, lc], 0.0) * e1a
                    w = term if w is None else w + term
                rows = slice(al * PEER_NKEYS, (al + 1) * PEER_NKEYS)
                cols = slice(lc * LANE, (lc + 1) * LANE)
                y = w * act_old[rows, cols].reshape(groups, *BF_TILE)
                y_new[rows, cols] = y.reshape(PEER_NKEYS, LANE)
        act_new[...] = _gelu(_dot(u_ref[...], x_ref[...])).astype(BF)

    @pl.when(s % 2 == 0)
    def _():
        step(act0, act1, y1, y0)

    @pl.when(s % 2 == 1)
    def _():
        step(act1, act0, y0, y1)


def _peer(hb_t, u_b, vt_b, c, p2, e1, e2, tm=1024):
    n = hb_t.shape[1]
    n_chunks, _, te = vt_b.shape
    last = n_chunks - 1
    rt = pl.BlockSpec((PEER_HEADS, tm // LANE, PEER_NKEYS, LANE), lambda i, s: (0, i, 0, 0),
                      pipeline_mode=pl.Buffered(1))
    packed = pltpu.VMEM((PEER_HEADS, tm // LANE, PEER_NKEYS // BF_TILE[0]) + BF_TILE, BF)
    chunk = pltpu.VMEM((te, tm), BF)
    return pl.pallas_call(
        functools.partial(_peer_kernel, n_chunks=n_chunks),
        grid=(n // tm, n_chunks + 2),
        in_specs=[pl.BlockSpec((D_MODEL, tm), lambda i, s: (0, i)),
                  pl.BlockSpec((te, D_MODEL), lambda i, s: (jnp.minimum(s, last), 0)),
                  pl.BlockSpec((1, D_MODEL, te), lambda i, s: (jnp.clip(s - 2, 0, last), 0, 0)),
                  rt, rt, rt, rt],
        out_specs=pl.BlockSpec((D_MODEL, tm), lambda i, s: (0, i)),
        out_shape=jax.ShapeDtypeStruct((D_MODEL, n), F32),
        scratch_shapes=[chunk, chunk, chunk, chunk, packed, packed],
        compiler_params=pltpu.CompilerParams(dimension_semantics=("arbitrary", "arbitrary"),
                                             vmem_limit_bytes=VMEM_LIMIT),
        name="peer",
    )(hb_t, u_b, vt_b, c, p2, e1, e2)


def _final_kernel(ft_ref, h_ref, p_ref, wg_ref, wp_ref, g_ref, b_ref, o_ref):
    ffn = ft_ref[...].T
    h = _layer_norm(DEEPNORM_ALPHA * h_ref[...] + ffn, g_ref[...], b_ref[...])
    gate = jax.nn.sigmoid(_dot(h.astype(BF), wg_ref[...]))
    o_ref[...] = h + gate * _dot(p_ref[...].astype(BF), wp_ref[...])


def _final(ffn_t, h1, p2d, wg, wp, g, b, tm=512):
    n = h1.shape[0]
    full = lambda a: pl.BlockSpec(a.shape, lambda i: (0, 0))
    return pl.pallas_call(
        _final_kernel,
        grid=(n // tm,),
        in_specs=[pl.BlockSpec((D_MODEL, tm), lambda i: (0, i)),
                  pl.BlockSpec((tm, D_MODEL), lambda i: (i, 0)),
                  pl.BlockSpec((tm, PLE_DIM), lambda i: (i, 0)),
                  full(wg), full(wp), full(g), full(b)],
        out_specs=pl.BlockSpec((tm, D_MODEL), lambda i: (i, 0)),
        out_shape=jax.ShapeDtypeStruct((n, D_MODEL), F32),
        compiler_params=pltpu.CompilerParams(dimension_semantics=("arbitrary",),
                                             vmem_limit_bytes=VMEM_LIMIT),
        name="final",
    )(ffn_t, h1, p2d, wg, wp, g, b)


def _arrange_w_in(w_in):
    cuts = [NSA_WIDTH, 6 * NSA_KV, 3 * NSA_HEADS, 2 * DIFF_QK, 2 * DIFF_QK, DIFF_WIDTH, D_MODEL, D_MODEL]
    offs = [0]
    for c in cuts:
        offs.append(offs[-1] + c)
    w_qn, w_kv, w_g, w_qd, w_kd, w_vd, w_ga, w_gb = [w_in[:, offs[k]:offs[k + 1]] for k in range(8)]
    d = w_in.shape[0]

    def widen(w):
        w = w.reshape(d, -1, HEAD_DIM)
        return jnp.pad(w, ((0, 0), (0, 0), (0, LANE - HEAD_DIM))).reshape(d, -1)

    kv = w_kv.reshape(d, 6, NSA_GROUPS, HEAD_DIM)
    both = lambda a, b: jnp.stack([kv[:, a], kv[:, b]], axis=2).reshape(d, NSA_GROUPS * LANE)
    w_g = w_g.reshape(d, NSA_GROUPS, 3 * NSA_HPG)
    w_g = jnp.pad(w_g, ((0, 0), (0, 0), (0, GATE_ROWS - 3 * NSA_HPG))).reshape(d, NSA_GROUPS * GATE_ROWS)
    pair = lambda w: w.reshape(d, 2, DIFF_HEADS, DIFF_DH).transpose(0, 2, 1, 3).reshape(d, 2 * DIFF_QK)
    cols = {"ksel": widen(kv[:, 2].reshape(d, NSA_KV)), "kwin": widen(kv[:, 4].reshape(d, NSA_KV)),
            "vsw": both(3, 5), "kvc": both(0, 1), "kd": widen(pair(w_kd)), "vd": w_vd, "ga": w_ga, "gb": w_gb}
    rows = {"qn": widen(w_qn * (HEAD_DIM ** -0.5)), "g": w_g,
            "qd": widen(pair(w_qd) * (DIFF_DH ** -0.5))}
    w_all = jnp.concatenate([cols[name] for name, _, _ in PROJ_COLS], axis=1).astype(BF)
    wt_all = jnp.concatenate([rows[name] for name, _, _ in PROJ_ROWS], axis=1).T.astype(BF)
    return w_all, wt_all


def _query_aug():
    slot = jnp.arange(LANE)
    on = ((slot == AUG) | (slot == AUG + 1)).astype(F32)
    nsa = jnp.exp2(-(jnp.arange(NSA_HEADS) + 1.0))
    dif = jnp.repeat(jnp.exp2(-(8.0 / DIFF_HEADS) * (jnp.arange(DIFF_HEADS) + 1.0)), 2)
    parts = {"qn": (nsa[:, None] * on).reshape(-1), "g": jnp.zeros((NSA_GROUPS * GATE_ROWS,), F32),
             "qd": (dif[:, None] * on).reshape(-1)}
    return jnp.concatenate([parts[name] for name, _, _ in PROJ_ROWS]).reshape(PROJ_HEIGHT, 1)


def _sel_weights_t():
    c0 = jnp.arange(N_CMP)[None, :] * CMP_STRIDE
    s0 = jnp.arange(LANE)[:, None] * SEL_BLOCK
    ov = jnp.minimum(c0 + CMP_BLOCK, s0 + SEL_BLOCK) - jnp.maximum(c0, s0)
    w = jnp.clip(ov, 0, None).astype(F32) / CMP_BLOCK
    return jnp.where(jnp.arange(LANE)[:, None] < N_SEL, w, 0.0).astype(BF)


def _token_mixers(x2, batch, seq, w_in, cmp_pos_k, cmp_pos_v, cmp_k_w1, cmp_k_w2, cmp_v_w1, cmp_v_w2,
                  lam_q1, lam_k1, lam_q2, lam_k2, diff_norm_g):
    w_all, wt_all = _arrange_w_in(w_in)
    ksel, kwin, vsw, kvc, kd, vd, ga, gb, qn_t, gates_t, qd_t = _proj(x2, w_all, wt_all, _query_aug(), seq)
    r = kvc.reshape(batch, seq, NSA_GROUPS, 2, HEAD_DIM).transpose(0, 2, 3, 1, 4)
    r = r.reshape(batch, NSA_GROUPS, 2, seq // CMP_STRIDE, CMP_STRIDE * HEAD_DIM)
    pos2 = lambda pe: pe.reshape(2, CMP_STRIDE * HEAD_DIM)
    cmp = _compress(r, pos2(cmp_pos_k), pos2(cmp_pos_v), cmp_k_w1.astype(BF), cmp_k_w2.astype(BF),
                    cmp_v_w1.astype(BF), cmp_v_w2.astype(BF))
    o_nsa = _nsa(qn_t, ksel, kwin, vsw, cmp, gates_t, _sel_weights_t(), batch, seq)
    lamv = jnp.stack([lam_q1, lam_k1, lam_q2, lam_k2]).astype(F32)
    o_diff = _diff(qd_t, kd, vd, lamv, diff_norm_g.reshape(DIFF_VDIM, 1).astype(F32), batch, seq)
    return o_nsa, o_diff, ga, gb


def kernel(x, p, w_in, cmp_pos_k, cmp_pos_v, cmp_k_w1, cmp_k_w2, cmp_v_w1, cmp_v_w2, lam_q1, lam_k1, lam_q2, lam_k2, diff_norm_g, w_branch_nsa, w_branch_diff, w_out, ln1_g, ln1_b, peer_wq, peer_subkeys1, peer_subkeys2, peer_u, peer_v, ln2_g, ln2_b, ple_w_proj, ple_w_gate):
    batch, seq, d = x.shape
    assert seq == N_CMP * CMP_STRIDE and seq == N_SEL * SEL_BLOCK and d == D_MODEL
    x2 = x.reshape(batch * seq, d)
    row = lambda v: v.reshape(1, -1).astype(F32)
    o_nsa, o_diff, ga, gb = _token_mixers(
        x2, batch, seq, w_in[0], cmp_pos_k[0], cmp_pos_v[0], cmp_k_w1[0], cmp_k_w2[0], cmp_v_w1[0],
        cmp_v_w2[0], lam_q1[0], lam_k1[0], lam_q2[0], lam_k2[0], diff_norm_g[0])
    h1, h1b = _merge(o_nsa, o_diff, ga, gb, x2, w_branch_nsa[0].astype(BF), w_branch_diff[0].astype(BF),
                     w_out[0].astype(BF), row(ln1_g[0]), row(ln1_b[0]))
    c, p2, e1, e2 = _route(h1b, peer_wq[0].T.astype(BF), peer_subkeys1[0].astype(BF),
                           peer_subkeys2[0].astype(BF))
    vt = peer_v[0].astype(BF).reshape(PEER_NEXPERTS // PEER_CHUNK, PEER_CHUNK, d).transpose(0, 2, 1)
    ffn_t = _peer(h1b, peer_u[0].astype(BF), vt, c, p2, e1, e2)
    out = _final(ffn_t, h1, p[0].reshape(batch * seq, PLE_DIM), ple_w_gate[0].astype(BF),
                 ple_w_proj[0].astype(BF), row(ln2_g[0]), row(ln2_b[0]))
    return out.reshape(batch, seq, d)
```

```python
import functools
import math

import jax
import jax.numpy as jnp
from jax import lax
from jax.experimental import pallas as pl
from jax.experimental.pallas import tpu as pltpu

D_MODEL = 1024
DEPTH = 1
PLE_DIM = 256

NSA_HEADS = 8
NSA_GROUPS = 2
NSA_HPG = NSA_HEADS // NSA_GROUPS
HEAD_DIM = 64
CMP_BLOCK = 32
CMP_STRIDE = 16
SEL_BLOCK = 64
SEL_TOPK = 8
WINDOW = 512
SEL_FORCE = 1.0e4

DIFF_HEADS = 4
DIFF_DH = 64
DIFF_VDIM = 2 * DIFF_DH

PEER_HEADS = 8
PEER_NKEYS = 128
PEER_NEXPERTS = PEER_NKEYS * PEER_NKEYS
PEER_DKEY = 256
PEER_TOPK = 16

NSA_WIDTH = NSA_HEADS * HEAD_DIM
NSA_KV = NSA_GROUPS * HEAD_DIM
DIFF_QK = DIFF_HEADS * DIFF_DH
DIFF_WIDTH = DIFF_HEADS * DIFF_VDIM

DEEPNORM_ALPHA = (2.0 * DEPTH) ** 0.25
LN_EPS = 1e-5

BF = jnp.bfloat16
F32 = jnp.float32
I32 = jnp.int32

LANE = 128
SUBLANES = 8
BF_TILE = (16, LANE)
Q_TILE_NSA = 512
Q_TILE_DIFF = 1024
N_CMP = 128
N_SEL = 32
PEER_CHUNK = 512
VMEM_LIMIT = 56 * 1024 * 1024

AUG = HEAD_DIM
SEL_LANE0 = AUG + 8
M_INIT = -1.0e30
MASKED = -(2.0 ** 100)
GATE_ROWS = 16

PROJ_COLS = (("ksel", NSA_GROUPS * LANE, BF), ("kwin", NSA_GROUPS * LANE, BF), ("vsw", NSA_GROUPS * LANE, BF),
             ("kvc", NSA_GROUPS * LANE, BF), ("kd", 2 * DIFF_HEADS * LANE, BF), ("vd", DIFF_WIDTH, BF),
             ("ga", D_MODEL, BF), ("gb", D_MODEL, BF))
PROJ_WIDTH = sum(w for _, w, _ in PROJ_COLS)
PROJ_ROWS = (("qn", NSA_HEADS * LANE, BF), ("g", NSA_GROUPS * GATE_ROWS, F32), ("qd", 2 * DIFF_HEADS * LANE, BF))
PROJ_HEIGHT = sum(r for _, r, _ in PROJ_ROWS)


def _dot(a, b):
    return jnp.dot(a, b, preferred_element_type=F32)


def _dot_nt(a, b):
    return lax.dot_general(a, b, (((1,), (1,)), ((), ())), preferred_element_type=F32)


def _dot_tn(a, b):
    return lax.dot_general(a, b, (((0,), (0,)), ((), ())), preferred_element_type=F32)


def _gelu(x):
    return 0.5 * x * (1.0 + lax.erf(x * (2.0 ** -0.5)))


def _layer_norm(y, g, b):
    mu = jnp.mean(y, axis=-1, keepdims=True)
    d = y - mu
    var = jnp.mean(d * d, axis=-1, keepdims=True)
    return d * lax.rsqrt(var + LN_EPS) * g + b


def _split_bf16(v):
    hi = v.astype(BF).astype(F32)
    return hi, v - hi


def _proj_kernel(x_ref, w_ref, wt_ref, qaug_ref, *o_refs, seq):
    i = pl.program_id(0)
    tm = x_ref.shape[0]
    x = x_ref[...].astype(BF)
    t = (i * tm + lax.broadcasted_iota(I32, (tm, LANE), 0)) & (seq - 1)
    lane = lax.broadcasted_iota(I32, (tm, LANE), 1)
    hi, lo = _split_bf16(t.astype(F32))
    k_aug = jnp.where(lane == AUG, hi, jnp.where(lane == AUG + 1, lo, 0.0))
    k_aug_sel = k_aug + jnp.where(lane - SEL_LANE0 == (t >> 6), 1.0, 0.0)
    extra = {"ksel": lambda: jnp.concatenate([k_aug_sel] * NSA_GROUPS, axis=1),
             "kwin": lambda: jnp.concatenate([k_aug] * NSA_GROUPS, axis=1),
             "kd": lambda: jnp.concatenate([k_aug] * (2 * DIFF_HEADS), axis=1)}
    c0 = 0
    for (name, width, dt), o_ref in zip(PROJ_COLS, o_refs):
        v = _dot(x, w_ref[:, c0:c0 + width])
        c0 += width
        if name in extra:
            v = v + extra[name]()
        if name in ("ga", "gb"):
            v = jax.nn.sigmoid(v)
        o_ref[...] = v.astype(dt)
    r0 = 0
    for (name, rows, dt), o_ref in zip(PROJ_ROWS, o_refs[len(PROJ_COLS):]):
        v = _dot_nt(wt_ref[r0:r0 + rows, :], x) + qaug_ref[r0:r0 + rows, :]
        r0 += rows
        if name == "g":
            v = jax.nn.sigmoid(v)
        o_ref[...] = v.astype(dt)


def _proj(x2, w_all, wt_all, qaug, seq, tm=256):
    n = x2.shape[0]
    assert seq & (seq - 1) == 0 and seq % tm == 0
    full = lambda a: pl.BlockSpec(a.shape, lambda i: (0, 0))
    return pl.pallas_call(
        functools.partial(_proj_kernel, seq=seq),
        grid=(n // tm,),
        in_specs=[pl.BlockSpec((tm, D_MODEL), lambda i: (i, 0)), full(w_all), full(wt_all), full(qaug)],
        out_specs=([pl.BlockSpec((tm, w), lambda i: (i, 0)) for _, w, _ in PROJ_COLS]
                   + [pl.BlockSpec((r, tm), lambda i: (0, i)) for _, r, _ in PROJ_ROWS]),
        out_shape=([jax.ShapeDtypeStruct((n, w), dt) for _, w, dt in PROJ_COLS]
                   + [jax.ShapeDtypeStruct((r, n), dt) for _, r, dt in PROJ_ROWS]),
        compiler_params=pltpu.CompilerParams(dimension_semantics=("arbitrary",),
                                             vmem_limit_bytes=VMEM_LIMIT),
        name="proj",
    )(x2, w_all, wt_all, qaug)


def _compress_kernel(r_ref, posk_ref, posv_ref, wk1_ref, wk2_ref, wv1_ref, wv2_ref, o_ref):
    half = CMP_STRIDE * HEAD_DIM
    for idx, (pos_ref, w1_ref, w2_ref) in enumerate(((posk_ref, wk1_ref, wk2_ref),
                                                     (posv_ref, wv1_ref, wv2_ref))):
        r = r_ref[0, 0, idx].astype(F32)
        ra = (r + pos_ref[0:1, :]).astype(BF)
        rb = (r + pos_ref[1:2, :]).astype(BF)
        p1 = _dot(ra, w1_ref[0:half, :])
        p2 = _dot(rb, w1_ref[half:2 * half, :])
        hid = p1 + pltpu.roll(p2, N_CMP - 1, 0)
        c = _dot(_gelu(hid).astype(BF), w2_ref[...])
        o_ref[0, 0, :, idx * LANE:idx * LANE + HEAD_DIM] = c.astype(BF)
    lane = lax.broadcasted_iota(I32, (N_CMP, HEAD_DIM), 1)
    c_end = lax.broadcasted_iota(I32, (N_CMP, HEAD_DIM), 0) * CMP_STRIDE + (CMP_BLOCK - 1)
    hi, lo = _split_bf16(c_end.astype(F32))
    o_ref[0, 0, :, AUG:LANE] = jnp.where(lane == 0, hi, jnp.where(lane == 1, lo, 0.0)).astype(BF)
    o_ref[0, 0, :, LANE + HEAD_DIM:2 * LANE] = jnp.zeros((N_CMP, HEAD_DIM), BF)


def _compress(r, posk, posv, wk1, wk2, wv1, wv2):
    b = r.shape[0]
    full = lambda a: pl.BlockSpec(a.shape, lambda i, j: (0,) * a.ndim)
    return pl.pallas_call(
        _compress_kernel,
        grid=(b, NSA_GROUPS),
        in_specs=[pl.BlockSpec((1, 1, 2, N_CMP, CMP_STRIDE * HEAD_DIM), lambda i, j: (i, j, 0, 0, 0)),
                  full(posk), full(posv), full(wk1), full(wk2), full(wv1), full(wv2)],
        out_specs=pl.BlockSpec((1, 1, N_CMP, 2 * LANE), lambda i, j: (i, j, 0, 0)),
        out_shape=jax.ShapeDtypeStruct((b, NSA_GROUPS, N_CMP, 2 * LANE), BF),
        compiler_params=pltpu.CompilerParams(dimension_semantics=("arbitrary", "arbitrary")),
        name="compress",
    )(r, posk, posv, wk1, wk2, wv1, wv2)


def _flash_step(carry, s, v):
    m, l, acc = carry
    m_new = jnp.maximum(m, jnp.max(s, axis=0, keepdims=True))
    a = jnp.exp(m - m_new)
    p = jnp.exp(s - m_new)
    l = a * l + jnp.sum(p, axis=0, keepdims=True)
    acc = a * acc + _dot_tn(v, p.astype(BF))
    return m_new, l, acc


def _flash_init(rows, dv):
    return (jnp.full((1, rows), M_INIT, F32), jnp.zeros((1, rows), F32), jnp.zeros((dv, rows), F32))


def _flash_out(carry):
    _, l, acc = carry
    return acc / jnp.maximum(l, 1e-30)


def _nsa_kernel(qt_ref, ksel_ref, kwin_ref, vsw_ref, cmp_ref, gt_ref, selwt_ref, o_ref):
    i = pl.program_id(2)
    tq = ck = Q_TILE_NSA
    rows = NSA_HPG * tq
    qt = qt_ref[...]
    q4 = jnp.concatenate([qt[h * LANE:(h + 1) * LANE, :] for h in range(NSA_HPG)], axis=1)
    t_lane = i * tq + (lax.broadcasted_iota(I32, (1, rows), 1) & (tq - 1))
    key_col = lax.broadcasted_iota(I32, (ck, 1), 0)
    causal = (i * ck + key_col) <= t_lane

    kc = cmp_ref[0, 0, :, 0:LANE]
    vc = cmp_ref[0, 0, :, LANE:LANE + HEAD_DIM]
    c_end = lax.broadcasted_iota(I32, (N_CMP, 1), 0) * CMP_STRIDE + (CMP_BLOCK - 1)
    s = jnp.where(c_end <= t_lane, _dot(kc, q4), MASKED)
    m = jnp.maximum(jnp.max(s, axis=0, keepdims=True), M_INIT)
    e = jnp.exp(s - m)
    p_cmp = e / jnp.maximum(jnp.sum(e, axis=0, keepdims=True), 1e-30)
    o_cmp = _dot_tn(vc, p_cmp.astype(BF))

    psum = p_cmp[:, 0:tq] + p_cmp[:, tq:2 * tq] + p_cmp[:, 2 * tq:3 * tq] + p_cmp[:, 3 * tq:4 * tq]
    p_hi = psum.astype(BF)
    p_lo = (psum - p_hi.astype(F32)).astype(BF)
    selwt = selwt_ref[...]
    imp = (_dot(selwt, p_hi) + _dot(selwt, p_lo))[0:N_SEL]
    jb = lax.broadcasted_iota(I32, (N_SEL, tq), 0)
    tq_pos = i * tq + lax.broadcasted_iota(I32, (N_SEL, tq), 1)
    cur = tq_pos >> 6
    forced = (jb == 0) | (jb == cur) | (jb == cur - 1)
    score = jnp.where(forced, SEL_FORCE, jnp.where(jb * SEL_BLOCK <= tq_pos, imp, -SEL_FORCE))
    unsel = jnp.full((N_SEL, tq), MASKED, F32)
    for _ in range(SEL_TOPK):
        mx = jnp.max(score, axis=0, keepdims=True)
        idx = jnp.min(jnp.where(score == mx, jb, N_SEL), axis=0, keepdims=True)
        hit = jb == idx
        unsel = jnp.where(hit, 0.0, unsel)
        score = jnp.where(hit, -jnp.inf, score)
    neg = jnp.concatenate([jnp.zeros((SEL_LANE0, tq), F32), unsel,
                           jnp.zeros((LANE - SEL_LANE0 - N_SEL, tq), F32)], axis=0)
    q4s = (q4.astype(F32) + jnp.concatenate([neg] * NSA_HPG, axis=1)).astype(BF)

    def chunk(ref, j, c0, width):
        start = pl.multiple_of(j * ck, ck)
        return ref[pl.ds(start, ck), c0:c0 + width]

    def sel_body(j, carry):
        return _flash_step(carry, _dot(chunk(ksel_ref, j, 0, LANE), q4s), chunk(vsw_ref, j, 0, HEAD_DIM))

    carry = lax.fori_loop(0, i, sel_body, _flash_init(rows, HEAD_DIM))
    s = jnp.where(causal, _dot(chunk(ksel_ref, i, 0, LANE), q4s), MASKED)
    o_slc = _flash_out(_flash_step(carry, s, chunk(vsw_ref, i, 0, HEAD_DIM)))

    carry = _flash_init(rows, HEAD_DIM)
    n_back = WINDOW // ck
    for back in range(n_back, 0, -1):
        j = jnp.maximum(i - back, 0)
        ok = i >= back
        if back == n_back:
            ok = ok & (t_lane - (j * ck + key_col) < WINDOW)
        s = jnp.where(ok, _dot(chunk(kwin_ref, j, 0, LANE), q4), MASKED)
        carry = _flash_step(carry, s, chunk(vsw_ref, j, HEAD_DIM, HEAD_DIM))
    s = jnp.where(causal, _dot(chunk(kwin_ref, i, 0, LANE), q4), MASKED)
    o_win = _flash_out(_flash_step(carry, s, chunk(vsw_ref, i, HEAD_DIM, HEAD_DIM)))

    gates = gt_ref[...]
    outs = []
    for h in range(NSA_HPG):
        c0 = h * tq
        outs.append(gates[3 * h:3 * h + 1] * o_cmp[:, c0:c0 + tq]
                    + gates[3 * h + 1:3 * h + 2] * o_slc[:, c0:c0 + tq]
                    + gates[3 * h + 2:3 * h + 3] * o_win[:, c0:c0 + tq])
    o_ref[...] = jnp.concatenate(outs, axis=0).T.astype(BF)


def _nsa(qn_t, ksel, kwin, vsw, cmp, gates_t, selwt, batch, seq):
    tq = Q_TILE_NSA
    assert WINDOW % tq == 0 and seq % tq == 0
    nq = seq // tq
    per_seq = pl.BlockSpec((seq, LANE), lambda b, g, i: (b, g))
    return pl.pallas_call(
        _nsa_kernel,
        grid=(batch, NSA_GROUPS, nq),
        in_specs=[pl.BlockSpec((NSA_HPG * LANE, tq), lambda b, g, i: (g, b * nq + i)),
                  per_seq, per_seq, per_seq,
                  pl.BlockSpec((1, 1, N_CMP, 2 * LANE), lambda b, g, i: (b, g, 0, 0)),
                  pl.BlockSpec((GATE_ROWS, tq), lambda b, g, i: (g, b * nq + i)),
                  pl.BlockSpec((LANE, LANE), lambda b, g, i: (0, 0))],
        out_specs=pl.BlockSpec((tq, NSA_HPG * HEAD_DIM), lambda b, g, i: (b * nq + i, g)),
        out_shape=jax.ShapeDtypeStruct((batch * seq, NSA_WIDTH), BF),
        compiler_params=pltpu.CompilerParams(dimension_semantics=("arbitrary",) * 3,
                                             vmem_limit_bytes=VMEM_LIMIT),
        name="nsa",
    )(qn_t, ksel, kwin, vsw, cmp, gates_t, selwt)


def _diff_kernel(qt_ref, k_ref, v_ref, lam_ref, ng_ref, o_ref):
    i = pl.program_id(2)
    tq = ck = Q_TILE_DIFF
    q1 = qt_ref[0:LANE, :]
    q2 = qt_ref[LANE:2 * LANE, :]
    lam_init = 0.8 - 0.6 * math.exp(-0.3 * 0)
    lv = lam_ref[...]
    lam = (jnp.exp(jnp.sum(lv[0:1] * lv[1:2], axis=-1, keepdims=True))
           - jnp.exp(jnp.sum(lv[2:3] * lv[3:4], axis=-1, keepdims=True)) + lam_init)
    t_lane = i * tq + lax.broadcasted_iota(I32, (1, tq), 1)
    causal = (i * ck + lax.broadcasted_iota(I32, (ck, 1), 0)) <= t_lane

    def chunk(j):
        start = pl.multiple_of(j * ck, ck)
        return (k_ref[pl.ds(start, ck), 0:LANE], k_ref[pl.ds(start, ck), LANE:2 * LANE],
                v_ref[pl.ds(start, ck), :])

    def body(j, carry):
        c1, c2 = carry
        k1, k2, v = chunk(j)
        return _flash_step(c1, _dot(k1, q1), v), _flash_step(c2, _dot(k2, q2), v)

    init = (_flash_init(tq, DIFF_VDIM), _flash_init(tq, DIFF_VDIM))
    c1, c2 = lax.fori_loop(0, i, body, init)
    k1, k2, v = chunk(i)
    c1 = _flash_step(c1, jnp.where(causal, _dot(k1, q1), MASKED), v)
    c2 = _flash_step(c2, jnp.where(causal, _dot(k2, q2), MASKED), v)
    o = _flash_out(c1) - lam * _flash_out(c2)
    o = o * lax.rsqrt(jnp.mean(o * o, axis=0, keepdims=True) + 1e-5) * ng_ref[...] * (1.0 - lam_init)
    o_ref[...] = o.T.astype(BF)


def _diff(qd_t, kd, vd, lamv, ng_col, batch, seq):
    tq = Q_TILE_DIFF
    nq = seq // tq
    return pl.pallas_call(
        _diff_kernel,
        grid=(batch, DIFF_HEADS, nq),
        in_specs=[pl.BlockSpec((2 * LANE, tq), lambda b, h, i: (h, b * nq + i)),
                  pl.BlockSpec((seq, 2 * LANE), lambda b, h, i: (b, h)),
                  pl.BlockSpec((seq, DIFF_VDIM), lambda b, h, i: (b, h)),
                  pl.BlockSpec((4, DIFF_DH), lambda b, h, i: (0, 0)),
                  pl.BlockSpec((DIFF_VDIM, 1), lambda b, h, i: (0, 0))],
        out_specs=pl.BlockSpec((tq, DIFF_VDIM), lambda b, h, i: (b * nq + i, h)),
        out_shape=jax.ShapeDtypeStruct((batch * seq, DIFF_WIDTH), BF),
        compiler_params=pltpu.CompilerParams(dimension_semantics=("arbitrary",) * 3,
                                             vmem_limit_bytes=VMEM_LIMIT),
        name="diff",
    )(qd_t, kd, vd, lamv, ng_col)


def _merge_kernel(on_ref, od_ref, ga_ref, gb_ref, x_ref, wn_ref, wd_ref, wo_ref, g_ref, b_ref,
                  h_ref, hb_ref):
    merged = (ga_ref[...].astype(F32) * _dot(on_ref[...], wn_ref[...])
              + gb_ref[...].astype(F32) * _dot(od_ref[...], wd_ref[...]))
    mix = _dot(merged.astype(BF), wo_ref[...])
    h = _layer_norm(DEEPNORM_ALPHA * x_ref[...] + mix, g_ref[...], b_ref[...])
    h_ref[...] = h
    hb_ref[...] = h.T.astype(BF)


def _merge(o_nsa, o_diff, ga, gb, x2, wn, wd, wo, g, b, tm=512):
    n = x2.shape[0]
    tok = lambda w: pl.BlockSpec((tm, w), lambda i: (i, 0))
    full = lambda a: pl.BlockSpec(a.shape, lambda i: (0, 0))
    return pl.pallas_call(
        _merge_kernel,
        grid=(n // tm,),
        in_specs=[tok(NSA_WIDTH), tok(DIFF_WIDTH), tok(D_MODEL), tok(D_MODEL), tok(D_MODEL),
                  full(wn), full(wd), full(wo), full(g), full(b)],
        out_specs=[tok(D_MODEL), pl.BlockSpec((D_MODEL, tm), lambda i: (0, i))],
        out_shape=[jax.ShapeDtypeStruct((n, D_MODEL), F32), jax.ShapeDtypeStruct((D_MODEL, n), BF)],
        compiler_params=pltpu.CompilerParams(dimension_semantics=("arbitrary",),
                                             vmem_limit_bytes=VMEM_LIMIT),
        name="merge",
    )(o_nsa, o_diff, ga, gb, x2, wn, wd, wo, g, b)


N_CAND = PEER_TOPK + 8 * (PEER_TOPK - 1)


def _extract_topk(s, k):
    nrow, w = s.shape
    rows = lax.broadcasted_iota(I32, (nrow, w), 0)
    krow = lax.broadcasted_iota(I32, (k, w), 0)
    pos = jnp.full((nrow, w), float(k), F32)
    vals = jnp.zeros((k, w), F32)
    for r in range(k):
        m = jnp.max(s, axis=0, keepdims=True)
        idx = jnp.min(jnp.where(s == m, rows, nrow), axis=0, keepdims=True)
        hit = rows == idx
        pos = jnp.where(hit, float(r), pos)
        s = jnp.where(hit, -jnp.inf, s)
        vals = jnp.where(krow == r, m, vals)
    return vals, pos


def _sort_pairs(n):
    pairs = []

    def merge(lo, hi, r):
        step = r * 2
        if step < hi - lo:
            merge(lo, hi, step)
            merge(lo + r, hi, step)
            pairs.extend((i, i + r) for i in range(lo + r, hi - r, step))
        else:
            pairs.append((lo, lo + r))

    def sort(lo, hi):
        if hi - lo >= 1:
            mid = lo + (hi - lo) // 2
            sort(lo, mid)
            sort(mid + 1, hi)
            merge(lo, hi, 1)

    sort(0, n - 1)
    return pairs


def _sorted_top16(s):
    n = PEER_TOPK
    v = [s[SUBLANES * i:SUBLANES * (i + 1)] for i in range(n)]
    for i, j in _sort_pairs(n):
        v[i], v[j] = jnp.maximum(v[i], v[j]), jnp.minimum(v[i], v[j])
    for shift in (4, 2, 1):
        other = [pltpu.roll(x, shift, 0) for x in v]
        t = [jnp.maximum(v[i], other[n - 1 - i]) for i in range(n)]
        d = n // 2
        while d:
            for i in range(n):
                if not i & d:
                    t[i], t[i + d] = jnp.maximum(t[i], t[i + d]), jnp.minimum(t[i], t[i + d])
            d //= 2
        v = t
    return v


def _route_kernel(h_ref, wq_ref, sk1_ref, sk2_ref, c_ref, p2_ref, e1_ref, e2_ref, q_scr):
    t = h_ref.shape[1]
    q_scr[...] = _dot(wq_ref[...], h_ref[...]).astype(BF)
    half = PEER_DKEY // 2

    crow = lax.broadcasted_iota(I32, (N_CAND, LANE), 0)
    ci = jnp.where(crow < PEER_TOPK, 0, ((crow - PEER_TOPK) >> 3) + 1)
    cj = jnp.where(crow < PEER_TOPK, crow, (crow - PEER_TOPK) & 7)
    cand_ok = (ci + 1) * (cj + 1) <= PEER_TOPK
    krow = lax.broadcasted_iota(I32, (PEER_TOPK, LANE), 0)
    sub = lax.broadcasted_iota(I32, (SUBLANES, LANE), 0)
    frow = lax.broadcasted_iota(I32, (PEER_NKEYS, LANE), 0)
    fi = jnp.maximum((frow >> 3) - 1, 0)
    fj = jnp.where(frow < 2 * SUBLANES, frow, frow & 7)
    cand_fast_ok = ((fi + 1) * (fj + 1) <= PEER_TOPK) | ((frow >> 3 == PEER_TOPK - 1) & (fj == 1))

    def tables(s1, s2):
        v1, pos1 = _extract_topk(s1, PEER_TOPK)
        v2, pos2 = _extract_topk(s2, PEER_TOPK)
        pieces = [v1[0:1] + v2]
        for r in range(1, PEER_TOPK):
            pieces.append(v1[r:r + 1] + v2[0:8])
        cand = jnp.where(cand_ok, jnp.concatenate(pieces, axis=0), -jnp.inf)
        _, cpos = _extract_topk(cand, PEER_TOPK)
        picked = cpos < PEER_TOPK
        top = v1[0:1] + v2[0:1]
        z = jnp.sum(jnp.where(picked, jnp.exp(cand - top), 0.0), axis=0, keepdims=True)
        pk = picked.astype(F32)
        cnt = jnp.zeros((PEER_TOPK, LANE), F32)
        cnt = jnp.where(krow == 0, jnp.sum(pk[0:PEER_TOPK], axis=0, keepdims=True), cnt)
        for r in range(1, PEER_TOPK):
            lo = PEER_TOPK + 8 * (r - 1)
            cnt = jnp.where(krow == r, jnp.sum(pk[lo:lo + 8], axis=0, keepdims=True), cnt)
        c = jnp.zeros((PEER_NKEYS, LANE), F32)
        for r in range(PEER_TOPK):
            c = jnp.where(pos1 == float(r), cnt[r:r + 1], c)
        return c, pos2, jnp.exp(s1 - v1[0:1]) / z, jnp.exp(s2 - v2[0:1])

    def tables_distinct(s1, s2):
        n = PEER_TOPK
        slabs = lambda s: [s[SUBLANES * i:SUBLANES * (i + 1)] for i in range(PEER_NKEYS // SUBLANES)]
        v1 = _sorted_top16(s1)
        v2 = _sorted_top16(s2)
        col = lambda v, base: functools.reduce(
            lambda acc, k: jnp.where(sub == k, v[base + k], acc), range(1, SUBLANES), v[base])
        v2_lo, v2_hi = col(v2, 0), col(v2, SUBLANES)
        last = jnp.where(sub == 1, v1[n - 1] + v2[0], v1[n - 2] + v2_lo)
        pieces = [v1[0] + v2_lo, v1[0] + v2_hi] + [v1[r] + v2_lo for r in range(1, n - 2)] + [last]
        cand = jnp.where(cand_fast_ok, jnp.concatenate(pieces, axis=0), -jnp.inf)
        first = [v1[0], v1[0]] + v1[1:n - 2] + [jnp.where(sub == 1, v1[n - 1], v1[n - 2])]
        top = v1[0] + v2[0]
        vals = _sorted_top16(cand)
        tau = vals[n - 1]
        z = functools.reduce(lambda acc, m: acc + jnp.exp(m[0:1] - top[0:1]), vals[1:],
                             jnp.exp(vals[0][0:1] - top[0:1]))
        count = lambda pred: jnp.sum(jnp.where(pred, 1.0, 0.0), axis=0, keepdims=True)
        ok = count(cand >= tau[0:1]) == n
        for k in range(n - 1):
            ok = (ok & (vals[k][0:1] > vals[k + 1][0:1]) & (v1[k][0:1] > v1[k + 1][0:1])
                  & (v2[k][0:1] > v2[k + 1][0:1]))
        ok = ok & (count(s1 >= v1[n - 1][0:1]) == n) & (count(s2 >= v2[n - 1][0:1]) == n)
        kept = [jnp.where(p >= tau, f, jnp.inf) for p, f in zip(jnp.split(cand, n, axis=0), first)]
        odd = jnp.min(jnp.where(sub == 1, kept[n - 1], jnp.inf), axis=0, keepdims=True)
        kept[n - 1] = jnp.where(sub == 1, jnp.inf, kept[n - 1])
        th_lo = functools.reduce(jnp.minimum, kept[2:], kept[0])
        th_lo = jnp.where(sub == 0, jnp.minimum(th_lo, odd), th_lo)
        th_hi = kept[1]
        theta = ([jnp.broadcast_to(th_lo[j:j + 1], (SUBLANES, LANE)) for j in range(SUBLANES)]
                 + [jnp.broadcast_to(th_hi[j:j + 1], (SUBLANES, LANE)) for j in range(SUBLANES)])
        c, p2 = [], []
        for x1, x2 in zip(slabs(s1), slabs(s2)):
            c.append(functools.reduce(lambda acc, k: jnp.where(x1 >= theta[k], k + 1.0, acc),
                                      range(n), jnp.zeros_like(x1)))
            p2.append(functools.reduce(lambda acc, k: jnp.where(v2[k] > x2, k + 1.0, acc),
                                       range(n), jnp.zeros_like(x2)))
        e1 = jnp.exp(s1 - v1[0][0:1]) / z
        e2 = jnp.exp(s2 - v2[0][0:1])
        return (jnp.concatenate(c, axis=0), jnp.concatenate(p2, axis=0), e1, e2), ok

    def head_body(hd, _):
        base = pl.multiple_of(hd * PEER_DKEY, PEER_DKEY)
        s1_all = _dot(sk1_ref[...], q_scr[pl.ds(base, half), :])
        s2_all = _dot(sk2_ref[...], q_scr[pl.ds(base + half, half), :])
        for lc in range(t // LANE):
            s1 = s1_all[:, lc * LANE:(lc + 1) * LANE]
            s2 = s2_all[:, lc * LANE:(lc + 1) * LANE]

            def store(vals):
                for ref, v in zip((c_ref, p2_ref, e1_ref, e2_ref), vals):
                    ref[hd, lc] = v

            vals, distinct = tables_distinct(s1, s2)
            store(vals)

            @pl.when(jnp.min(jnp.where(distinct, 1.0, 0.0)) < 0.5)
            def _():
                store(tables(s1, s2))
        return 0

    lax.fori_loop(0, PEER_HEADS, head_body, 0)


def _route(hb_t, wq_t, sk1, sk2, tm=256):
    n = hb_t.shape[1]
    full = lambda a: pl.BlockSpec(a.shape, lambda i: (0, 0))
    out_spec = pl.BlockSpec((PEER_HEADS, tm // LANE, PEER_NKEYS, LANE), lambda i: (0, i, 0, 0))
    shape = (PEER_HEADS, n // LANE, PEER_NKEYS, LANE)
    return pl.pallas_call(
        _route_kernel,
        grid=(n // tm,),
        in_specs=[pl.BlockSpec((D_MODEL, tm), lambda i: (0, i)), full(wq_t), full(sk1), full(sk2)],
        out_specs=[out_spec] * 4,
        out_shape=[jax.ShapeDtypeStruct(shape, F32)] * 4,
        scratch_shapes=[pltpu.VMEM((PEER_HEADS * PEER_DKEY, tm), BF)],
        compiler_params=pltpu.CompilerParams(dimension_semantics=("arbitrary",),
                                             vmem_limit_bytes=VMEM_LIMIT),
        name="route",
    )(hb_t, wq_t, sk1, sk2)


def _peer_kernel(x_ref, u_ref, vt_ref, c_ref, p2_ref, e1_ref, e2_ref, o_ref,
                 act0, act1, y0, y1, p2_scr, e2_scr, *, n_chunks):
    s = pl.program_id(1)
    te, tm = y0.shape
    n_a = te // PEER_NKEYS
    groups = PEER_NKEYS // BF_TILE[0]

    @pl.when(s == 0)
    def _():
        o_ref[...] = jnp.zeros_like(o_ref)
        for r in (act0, act1, y0, y1):
            r[...] = jnp.zeros_like(r)
        for hd in range(PEER_HEADS):
            for lc in range(tm // LANE):
                p2_scr[hd, lc] = p2_ref[hd, lc].astype(BF).reshape(groups, *BF_TILE)
                e2_scr[hd, lc] = e2_ref[hd, lc].astype(BF).reshape(groups, *BF_TILE)

    jb = jnp.clip(s - 1, 0, n_chunks - 1)

    def step(act_new, act_old, y_new, y_old):
        o_ref[...] += _dot(vt_ref[0], y_old[...])
        for al in range(n_a):
            a = jb * n_a + al
            for lc in range(tm // LANE):
                w = None
                for hd in range(PEER_HEADS):
                    ca = jnp.broadcast_to(c_ref[hd, lc, pl.ds(a, 1), :], BF_TILE).astype(BF)[None]
                    e1a = jnp.broadcast_to(e1_ref[hd, lc, pl.ds(a, 1), :], BF_TILE).astype(BF)[None]
                    term = jnp.where(p2_scr[hd, lc] < ca, e2_scr[hd, lc], 0.0) * e1a
                    w = term if w is None else w + term
                rows = slice(al * PEER_NKEYS, (al + 1) * PEER_NKEYS)
                cols = slice(lc * LANE, (lc + 1) * LANE)
                y = w * act_old[rows, cols].reshape(groups, *BF_TILE)
                y_new[rows, cols] = y.reshape(PEER_NKEYS, LANE)
        act_new[...] = _gelu(_dot(u_ref[...], x_ref[...])).astype(BF)

    @pl.when(s % 2 == 0)
    def _():
        step(act0, act1, y1, y0)

    @pl.when(s % 2 == 1)
    def _():
        step(act1, act0, y0, y1)


def _peer(hb_t, u_b, vt_b, c, p2, e1, e2, tm=1024):
    n = hb_t.shape[1]
    n_chunks, _, te = vt_b.shape
    last = n_chunks - 1
    rt = pl.BlockSpec((PEER_HEADS, tm // LANE, PEER_NKEYS, LANE), lambda i, s: (0, i, 0, 0),
                      pipeline_mode=pl.Buffered(1))
    packed = pltpu.VMEM((PEER_HEADS, tm // LANE, PEER_NKEYS // BF_TILE[0]) + BF_TILE, BF)
    chunk = pltpu.VMEM((te, tm), BF)
    return pl.pallas_call(
        functools.partial(_peer_kernel, n_chunks=n_chunks),
        grid=(n // tm, n_chunks + 2),
        in_specs=[pl.BlockSpec((D_MODEL, tm), lambda i, s: (0, i)),
                  pl.BlockSpec((te, D_MODEL), lambda i, s: (jnp.minimum(s, last), 0)),
                  pl.BlockSpec((1, D_MODEL, te), lambda i, s: (jnp.clip(s - 2, 0, last), 0, 0)),
                  rt, rt, rt, rt],
        out_specs=pl.BlockSpec((D_MODEL, tm), lambda i, s: (0, i)),
        out_shape=jax.ShapeDtypeStruct((D_MODEL, n), F32),
        scratch_shapes=[chunk, chunk, chunk, chunk, packed, packed],
        compiler_params=pltpu.CompilerParams(dimension_semantics=("arbitrary", "arbitrary"),
                                             vmem_limit_bytes=VMEM_LIMIT),
        name="peer",
    )(hb_t, u_b, vt_b, c, p2, e1, e2)


def _final_kernel(ft_ref, h_ref, p_ref, wg_ref, wp_ref, g_ref, b_ref, o_ref):
    ffn = ft_ref[...].T
    h = _layer_norm(DEEPNORM_ALPHA * h_ref[...] + ffn, g_ref[...], b_ref[...])
    gate = jax.nn.sigmoid(_dot(h.astype(BF), wg_ref[...]))
    o_ref[...] = h + gate * _dot(p_ref[...].astype(BF), wp_ref[...])


def _final(ffn_t, h1, p2d, wg, wp, g, b, tm=512):
    n = h1.shape[0]
    full = lambda a: pl.BlockSpec(a.shape, lambda i: (0, 0))
    return pl.pallas_call(
        _final_kernel,
        grid=(n // tm,),
        in_specs=[pl.BlockSpec((D_MODEL, tm), lambda i: (0, i)),
                  pl.BlockSpec((tm, D_MODEL), lambda i: (i, 0)),
                  pl.BlockSpec((tm, PLE_DIM), lambda i: (i, 0)),
                  full(wg), full(wp), full(g), full(b)],
        out_specs=pl.BlockSpec((tm, D_MODEL), lambda i: (i, 0)),
        out_shape=jax.ShapeDtypeStruct((n, D_MODEL), F32),
        compiler_params=pltpu.CompilerParams(dimension_semantics=("arbitrary",),
                                             vmem_limit_bytes=VMEM_LIMIT),
        name="final",
    )(ffn_t, h1, p2d, wg, wp, g, b)


def _arrange_w_in(w_in):
    cuts = [NSA_WIDTH, 6 * NSA_KV, 3 * NSA_HEADS, 2 * DIFF_QK, 2 * DIFF_QK, DIFF_WIDTH, D_MODEL, D_MODEL]
    offs = [0]
    for c in cuts:
        offs.append(offs[-1] + c)
    w_qn, w_kv, w_g, w_qd, w_kd, w_vd, w_ga, w_gb = [w_in[:, offs[k]:offs[k + 1]] for k in range(8)]
    d = w_in.shape[0]

    def widen(w):
        w = w.reshape(d, -1, HEAD_DIM)
        return jnp.pad(w, ((0, 0), (0, 0), (0, LANE - HEAD_DIM))).reshape(d, -1)

    kv = w_kv.reshape(d, 6, NSA_GROUPS, HEAD_DIM)
    both = lambda a, b: jnp.stack([kv[:, a], kv[:, b]], axis=2).reshape(d, NSA_GROUPS * LANE)
    w_g = w_g.reshape(d, NSA_GROUPS, 3 * NSA_HPG)
    w_g = jnp.pad(w_g, ((0, 0), (0, 0), (0, GATE_ROWS - 3 * NSA_HPG))).reshape(d, NSA_GROUPS * GATE_ROWS)
    pair = lambda w: w.reshape(d, 2, DIFF_HEADS, DIFF_DH).transpose(0, 2, 1, 3).reshape(d, 2 * DIFF_QK)
    cols = {"ksel": widen(kv[:, 2].reshape(d, NSA_KV)), "kwin": widen(kv[:, 4].reshape(d, NSA_KV)),
            "vsw": both(3, 5), "kvc": both(0, 1), "kd": widen(pair(w_kd)), "vd": w_vd, "ga": w_ga, "gb": w_gb}
    rows = {"qn": widen(w_qn * (HEAD_DIM ** -0.5)), "g": w_g,
            "qd": widen(pair(w_qd) * (DIFF_DH ** -0.5))}
    w_all = jnp.concatenate([cols[name] for name, _, _ in PROJ_COLS], axis=1).astype(BF)
    wt_all = jnp.concatenate([rows[name] for name, _, _ in PROJ_ROWS], axis=1).T.astype(BF)
    return w_all, wt_all


def _query_aug():
    slot = jnp.arange(LANE)
    on = ((slot == AUG) | (slot == AUG + 1)).astype(F32)
    nsa = jnp.exp2(-(jnp.arange(NSA_HEADS) + 1.0))
    dif = jnp.repeat(jnp.exp2(-(8.0 / DIFF_HEADS) * (jnp.arange(DIFF_HEADS) + 1.0)), 2)
    parts = {"qn": (nsa[:, None] * on).reshape(-1), "g": jnp.zeros((NSA_GROUPS * GATE_ROWS,), F32),
             "qd": (dif[:, None] * on).reshape(-1)}
    return jnp.concatenate([parts[name] for name, _, _ in PROJ_ROWS]).reshape(PROJ_HEIGHT, 1)


def _sel_weights_t():
    c0 = jnp.arange(N_CMP)[None, :] * CMP_STRIDE
    s0 = jnp.arange(LANE)[:, None] * SEL_BLOCK
    ov = jnp.minimum(c0 + CMP_BLOCK, s0 + SEL_BLOCK) - jnp.maximum(c0, s0)
    w = jnp.clip(ov, 0, None).astype(F32) / CMP_BLOCK
    return jnp.where(jnp.arange(LANE)[:, None] < N_SEL, w, 0.0).astype(BF)


def _token_mixers(x2, batch, seq, w_in, cmp_pos_k, cmp_pos_v, cmp_k_w1, cmp_k_w2, cmp_v_w1, cmp_v_w2,
                  lam_q1, lam_k1, lam_q2, lam_k2, diff_norm_g):
    w_all, wt_all = _arrange_w_in(w_in)
    ksel, kwin, vsw, kvc, kd, vd, ga, gb, qn_t, gates_t, qd_t = _proj(x2, w_all, wt_all, _query_aug(), seq)
    r = kvc.reshape(batch, seq, NSA_GROUPS, 2, HEAD_DIM).transpose(0, 2, 3, 1, 4)
    r = r.reshape(batch, NSA_GROUPS, 2, seq // CMP_STRIDE, CMP_STRIDE * HEAD_DIM)
    pos2 = lambda pe: pe.reshape(2, CMP_STRIDE * HEAD_DIM)
    cmp = _compress(r, pos2(cmp_pos_k), pos2(cmp_pos_v), cmp_k_w1.astype(BF), cmp_k_w2.astype(BF),
                    cmp_v_w1.astype(BF), cmp_v_w2.astype(BF))
    o_nsa = _nsa(qn_t, ksel, kwin, vsw, cmp, gates_t, _sel_weights_t(), batch, seq)
    lamv = jnp.stack([lam_q1, lam_k1, lam_q2, lam_k2]).astype(F32)
    o_diff = _diff(qd_t, kd, vd, lamv, diff_norm_g.reshape(DIFF_VDIM, 1).astype(F32), batch, seq)
    return o_nsa, o_diff, ga, gb


def kernel(x, p, w_in, cmp_pos_k, cmp_pos_v, cmp_k_w1, cmp_k_w2, cmp_v_w1, cmp_v_w2, lam_q1, lam_k1, lam_q2, lam_k2, diff_norm_g, w_branch_nsa, w_branch_diff, w_out, ln1_g, ln1_b, peer_wq, peer_subkeys1, peer_subkeys2, peer_u, peer_v, ln2_g, ln2_b, ple_w_proj, ple_w_gate):
    batch, seq, d = x.shape
    assert seq == N_CMP * CMP_STRIDE and seq == N_SEL * SEL_BLOCK and d == D_MODEL
    x2 = x.reshape(batch * seq, d)
    row = lambda v: v.reshape(1, -1).astype(F32)
    o_nsa, o_diff, ga, gb = _token_mixers(
        x2, batch, seq, w_in[0], cmp_pos_k[0], cmp_pos_v[0], cmp_k_w1[0], cmp_k_w2[0], cmp_v_w1[0],
        cmp_v_w2[0], lam_q1[0], lam_k1[0], lam_q2[0], lam_k2[0], diff_norm_g[0])
    h1, h1b = _merge(o_nsa, o_diff, ga, gb, x2, w_branch_nsa[0].astype(BF), w_branch_diff[0].astype(BF),
                     w_out[0].astype(BF), row(ln1_g[0]), row(ln1_b[0]))
    c, p2, e1, e2 = _route(h1b, peer_wq[0].T.astype(BF), peer_subkeys1[0].astype(BF),
                           peer_subkeys2[0].astype(BF))
    vt = peer_v[0].astype(BF).reshape(PEER_NEXPERTS // PEER_CHUNK, PEER_CHUNK, d).transpose(0, 2, 1)
    ffn_t = _peer(h1b, peer_u[0].astype(BF), vt, c, p2, e1, e2)
    out = _final(ffn_t, h1, p[0].reshape(batch * seq, PLE_DIM), ple_w_gate[0].astype(BF),
                 ple_w_proj[0].astype(BF), row(ln2_g[0]), row(ln2_b[0]))
    return out.reshape(batch, seq, d)
```

```python
import functools
import math

import jax
import jax.numpy as jnp
from jax import lax
from jax.experimental import pallas as pl
from jax.experimental.pallas import tpu as pltpu

D_MODEL = 1024
DEPTH = 1
PLE_DIM = 256

NSA_HEADS = 8
NSA_GROUPS = 2
NSA_HPG = NSA_HEADS // NSA_GROUPS
HEAD_DIM = 64
CMP_BLOCK = 32
CMP_STRIDE = 16
SEL_BLOCK = 64
SEL_TOPK = 8
WINDOW = 512
SEL_FORCE = 1.0e4

DIFF_HEADS = 4
DIFF_DH = 64
DIFF_VDIM = 2 * DIFF_DH

PEER_HEADS = 8
PEER_NKEYS = 128
PEER_NEXPERTS = PEER_NKEYS * PEER_NKEYS
PEER_DKEY = 256
PEER_TOPK = 16

NSA_WIDTH = NSA_HEADS * HEAD_DIM
NSA_KV = NSA_GROUPS * HEAD_DIM
DIFF_QK = DIFF_HEADS * DIFF_DH
DIFF_WIDTH = DIFF_HEADS * DIFF_VDIM

DEEPNORM_ALPHA = (2.0 * DEPTH) ** 0.25
LN_EPS = 1e-5

BF = jnp.bfloat16
F32 = jnp.float32
I32 = jnp.int32

LANE = 128
SUBLANES = 8
BF_TILE = (16, LANE)
Q_TILE_NSA = 512
Q_TILE_DIFF = 1024
N_CMP = 128
N_SEL = 32
PEER_CHUNK = 512
VMEM_LIMIT = 56 * 1024 * 1024

AUG = HEAD_DIM
SEL_LANE0 = AUG + 8
M_INIT = -1.0e30
MASKED = -(2.0 ** 100)
GATE_ROWS = 16

PROJ_COLS = (("ksel", NSA_GROUPS * LANE, BF), ("kwin", NSA_GROUPS * LANE, BF), ("vsw", NSA_GROUPS * LANE, BF),
             ("kvc", NSA_GROUPS * LANE, BF), ("kd", 2 * DIFF_HEADS * LANE, BF), ("vd", DIFF_WIDTH, BF),
             ("ga", D_MODEL, BF), ("gb", D_MODEL, BF))
PROJ_WIDTH = sum(w for _, w, _ in PROJ_COLS)
PROJ_ROWS = (("qn", NSA_HEADS * LANE, BF), ("g", NSA_GROUPS * GATE_ROWS, F32), ("qd", 2 * DIFF_HEADS * LANE, BF))
PROJ_HEIGHT = sum(r for _, r, _ in PROJ_ROWS)


def _dot(a, b):
    return jnp.dot(a, b, preferred_element_type=F32)


def _dot_nt(a, b):
    return lax.dot_general(a, b, (((1,), (1,)), ((), ())), preferred_element_type=F32)


def _dot_tn(a, b):
    return lax.dot_general(a, b, (((0,), (0,)), ((), ())), preferred_element_type=F32)


def _gelu(x):
    return 0.5 * x * (1.0 + lax.erf(x * (2.0 ** -0.5)))


def _layer_norm(y, g, b):
    mu = jnp.mean(y, axis=-1, keepdims=True)
    d = y - mu
    var = jnp.mean(d * d, axis=-1, keepdims=True)
    return d * lax.rsqrt(var + LN_EPS) * g + b


def _split_bf16(v):
    hi = v.astype(BF).astype(F32)
    return hi, v - hi


def _proj_kernel(x_ref, w_ref, wt_ref, qaug_ref, *o_refs, seq):
    i = pl.program_id(0)
    tm = x_ref.shape[0]
    x = x_ref[...].astype(BF)
    t = (i * tm + lax.broadcasted_iota(I32, (tm, LANE), 0)) & (seq - 1)
    lane = lax.broadcasted_iota(I32, (tm, LANE), 1)
    hi, lo = _split_bf16(t.astype(F32))
    k_aug = jnp.where(lane == AUG, hi, jnp.where(lane == AUG + 1, lo, 0.0))
    k_aug_sel = k_aug + jnp.where(lane - SEL_LANE0 == (t >> 6), 1.0, 0.0)
    extra = {"ksel": lambda: jnp.concatenate([k_aug_sel] * NSA_GROUPS, axis=1),
             "kwin": lambda: jnp.concatenate([k_aug] * NSA_GROUPS, axis=1),
             "kd": lambda: jnp.concatenate([k_aug] * (2 * DIFF_HEADS), axis=1)}
    c0 = 0
    for (name, width, dt), o_ref in zip(PROJ_COLS, o_refs):
        v = _dot(x, w_ref[:, c0:c0 + width])
        c0 += width
        if name in extra:
            v = v + extra[name]()
        if name in ("ga", "gb"):
            v = jax.nn.sigmoid(v)
        o_ref[...] = v.astype(dt)
    r0 = 0
    for (name, rows, dt), o_ref in zip(PROJ_ROWS, o_refs[len(PROJ_COLS):]):
        v = _dot_nt(wt_ref[r0:r0 + rows, :], x) + qaug_ref[r0:r0 + rows, :]
        r0 += rows
        if name == "g":
            v = jax.nn.sigmoid(v)
        o_ref[...] = v.astype(dt)


def _proj(x2, w_all, wt_all, qaug, seq, tm=256):
    n = x2.shape[0]
    assert seq & (seq - 1) == 0 and seq % tm == 0
    full = lambda a: pl.BlockSpec(a.shape, lambda i: (0, 0))
    return pl.pallas_call(
        functools.partial(_proj_kernel, seq=seq),
        grid=(n // tm,),
        in_specs=[pl.BlockSpec((tm, D_MODEL), lambda i: (i, 0)), full(w_all), full(wt_all), full(qaug)],
        out_specs=([pl.BlockSpec((tm, w), lambda i: (i, 0)) for _, w, _ in PROJ_COLS]
                   + [pl.BlockSpec((r, tm), lambda i: (0, i)) for _, r, _ in PROJ_ROWS]),
        out_shape=([jax.ShapeDtypeStruct((n, w), dt) for _, w, dt in PROJ_COLS]
                   + [jax.ShapeDtypeStruct((r, n), dt) for _, r, dt in PROJ_ROWS]),
        compiler_params=pltpu.CompilerParams(dimension_semantics=("arbitrary",),
                                             vmem_limit_bytes=VMEM_LIMIT),
        name="proj",
    )(x2, w_all, wt_all, qaug)


def _compress_kernel(r_ref, posk_ref, posv_ref, wk1_ref, wk2_ref, wv1_ref, wv2_ref, o_ref):
    half = CMP_STRIDE * HEAD_DIM
    for idx, (pos_ref, w1_ref, w2_ref) in enumerate(((posk_ref, wk1_ref, wk2_ref),
                                                     (posv_ref, wv1_ref, wv2_ref))):
        r = r_ref[0, 0, idx].astype(F32)
        ra = (r + pos_ref[0:1, :]).astype(BF)
        rb = (r + pos_ref[1:2, :]).astype(BF)
        p1 = _dot(ra, w1_ref[0:half, :])
        p2 = _dot(rb, w1_ref[half:2 * half, :])
        hid = p1 + pltpu.roll(p2, N_CMP - 1, 0)
        c = _dot(_gelu(hid).astype(BF), w2_ref[...])
        o_ref[0, 0, :, idx * LANE:idx * LANE + HEAD_DIM] = c.astype(BF)
    lane = lax.broadcasted_iota(I32, (N_CMP, HEAD_DIM), 1)
    c_end = lax.broadcasted_iota(I32, (N_CMP, HEAD_DIM), 0) * CMP_STRIDE + (CMP_BLOCK - 1)
    hi, lo = _split_bf16(c_end.astype(F32))
    o_ref[0, 0, :, AUG:LANE] = jnp.where(lane == 0, hi, jnp.where(lane == 1, lo, 0.0)).astype(BF)
    o_ref[0, 0, :, LANE + HEAD_DIM:2 * LANE] = jnp.zeros((N_CMP, HEAD_DIM), BF)


def _compress(r, posk, posv, wk1, wk2, wv1, wv2):
    b = r.shape[0]
    full = lambda a: pl.BlockSpec(a.shape, lambda i, j: (0,) * a.ndim)
    return pl.pallas_call(
        _compress_kernel,
        grid=(b, NSA_GROUPS),
        in_specs=[pl.BlockSpec((1, 1, 2, N_CMP, CMP_STRIDE * HEAD_DIM), lambda i, j: (i, j, 0, 0, 0)),
                  full(posk), full(posv), full(wk1), full(wk2), full(wv1), full(wv2)],
        out_specs=pl.BlockSpec((1, 1, N_CMP, 2 * LANE), lambda i, j: (i, j, 0, 0)),
        out_shape=jax.ShapeDtypeStruct((b, NSA_GROUPS, N_CMP, 2 * LANE), BF),
        compiler_params=pltpu.CompilerParams(dimension_semantics=("arbitrary", "arbitrary")),
        name="compress",
    )(r, posk, posv, wk1, wk2, wv1, wv2)


def _flash_step(carry, s, v):
    m, l, acc = carry
    m_new = jnp.maximum(m, jnp.max(s, axis=0, keepdims=True))
    a = jnp.exp(m - m_new)
    p = jnp.exp(s - m_new)
    l = a * l + jnp.sum(p, axis=0, keepdims=True)
    acc = a * acc + _dot_tn(v, p.astype(BF))
    return m_new, l, acc


def _flash_init(rows, dv):
    return (jnp.full((1, rows), M_INIT, F32), jnp.zeros((1, rows), F32), jnp.zeros((dv, rows), F32))


def _flash_out(carry):
    _, l, acc = carry
    return acc / jnp.maximum(l, 1e-30)


def _nsa_kernel(qt_ref, ksel_ref, kwin_ref, vsw_ref, cmp_ref, gt_ref, selwt_ref, o_ref):
    i = pl.program_id(2)
    tq = ck = Q_TILE_NSA
    rows = NSA_HPG * tq
    qt = qt_ref[...]
    q4 = jnp.concatenate([qt[h * LANE:(h + 1) * LANE, :] for h in range(NSA_HPG)], axis=1)
    t_lane = i * tq + (lax.broadcasted_iota(I32, (1, rows), 1) & (tq - 1))
    key_col = lax.broadcasted_iota(I32, (ck, 1), 0)
    causal = (i * ck + key_col) <= t_lane

    kc = cmp_ref[0, 0, :, 0:LANE]
    vc = cmp_ref[0, 0, :, LANE:LANE + HEAD_DIM]
    c_end = lax.broadcasted_iota(I32, (N_CMP, 1), 0) * CMP_STRIDE + (CMP_BLOCK - 1)
    s = jnp.where(c_end <= t_lane, _dot(kc, q4), MASKED)
    m = jnp.maximum(jnp.max(s, axis=0, keepdims=True), M_INIT)
    e = jnp.exp(s - m)
    p_cmp = e / jnp.maximum(jnp.sum(e, axis=0, keepdims=True), 1e-30)
    o_cmp = _dot_tn(vc, p_cmp.astype(BF))

    psum = p_cmp[:, 0:tq] + p_cmp[:, tq:2 * tq] + p_cmp[:, 2 * tq:3 * tq] + p_cmp[:, 3 * tq:4 * tq]
    p_hi = psum.astype(BF)
    p_lo = (psum - p_hi.astype(F32)).astype(BF)
    selwt = selwt_ref[...]
    imp = (_dot(selwt, p_hi) + _dot(selwt, p_lo))[0:N_SEL]
    jb = lax.broadcasted_iota(I32, (N_SEL, tq), 0)
    tq_pos = i * tq + lax.broadcasted_iota(I32, (N_SEL, tq), 1)
    cur = tq_pos >> 6
    forced = (jb == 0) | (jb == cur) | (jb == cur - 1)
    score = jnp.where(forced, SEL_FORCE, jnp.where(jb * SEL_BLOCK <= tq_pos, imp, -SEL_FORCE))
    unsel = jnp.full((N_SEL, tq), MASKED, F32)
    for _ in range(SEL_TOPK):
        mx = jnp.max(score, axis=0, keepdims=True)
        idx = jnp.min(jnp.where(score == mx, jb, N_SEL), axis=0, keepdims=True)
        hit = jb == idx
        unsel = jnp.where(hit, 0.0, unsel)
        score = jnp.where(hit, -jnp.inf, score)
    neg = jnp.concatenate([jnp.zeros((SEL_LANE0, tq), F32), unsel,
                           jnp.zeros((LANE - SEL_LANE0 - N_SEL, tq), F32)], axis=0)
    q4s = (q4.astype(F32) + jnp.concatenate([neg] * NSA_HPG, axis=1)).astype(BF)

    def chunk(ref, j, c0, width):
        start = pl.multiple_of(j * ck, ck)
        return ref[pl.ds(start, ck), c0:c0 + width]

    def sel_body(j, carry):
        return _flash_step(carry, _dot(chunk(ksel_ref, j, 0, LANE), q4s), chunk(vsw_ref, j, 0, HEAD_DIM))

    carry = lax.fori_loop(0, i, sel_body, _flash_init(rows, HEAD_DIM))
    s = jnp.where(causal, _dot(chunk(ksel_ref, i, 0, LANE), q4s), MASKED)
    o_slc = _flash_out(_flash_step(carry, s, chunk(vsw_ref, i, 0, HEAD_DIM)))

    carry = _flash_init(rows, HEAD_DIM)
    n_back = WINDOW // ck
    for back in range(n_back, 0, -1):
        j = jnp.maximum(i - back, 0)
        ok = i >= back
        if back == n_back:
            ok = ok & (t_lane - (j * ck + key_col) < WINDOW)
        s = jnp.where(ok, _dot(chunk(kwin_ref, j, 0, LANE), q4), MASKED)
        carry = _flash_step(carry, s, chunk(vsw_ref, j, HEAD_DIM, HEAD_DIM))
    s = jnp.where(causal, _dot(chunk(kwin_ref, i, 0, LANE), q4), MASKED)
    o_win = _flash_out(_flash_step(carry, s, chunk(vsw_ref, i, HEAD_DIM, HEAD_DIM)))

    gates = gt_ref[...]
    outs = []
    for h in range(NSA_HPG):
        c0 = h * tq
        outs.append(gates[3 * h:3 * h + 1] * o_cmp[:, c0:c0 + tq]
                    + gates[3 * h + 1:3 * h + 2] * o_slc[:, c0:c0 + tq]
                    + gates[3 * h + 2:3 * h + 3] * o_win[:, c0:c0 + tq])
    o_ref[...] = jnp.concatenate(outs, axis=0).T.astype(BF)


def _nsa(qn_t, ksel, kwin, vsw, cmp, gates_t, selwt, batch, seq):
    tq = Q_TILE_NSA
    assert WINDOW % tq == 0 and seq % tq == 0
    nq = seq // tq
    per_seq = pl.BlockSpec((seq, LANE), lambda b, g, i: (b, g))
    return pl.pallas_call(
        _nsa_kernel,
        grid=(batch, NSA_GROUPS, nq),
        in_specs=[pl.BlockSpec((NSA_HPG * LANE, tq), lambda b, g, i: (g, b * nq + i)),
                  per_seq, per_seq, per_seq,
                  pl.BlockSpec((1, 1, N_CMP, 2 * LANE), lambda b, g, i: (b, g, 0, 0)),
                  pl.BlockSpec((GATE_ROWS, tq), lambda b, g, i: (g, b * nq + i)),
                  pl.BlockSpec((LANE, LANE), lambda b, g, i: (0, 0))],
        out_specs=pl.BlockSpec((tq, NSA_HPG * HEAD_DIM), lambda b, g, i: (b * nq + i, g)),
        out_shape=jax.ShapeDtypeStruct((batch * seq, NSA_WIDTH), BF),
        compiler_params=pltpu.CompilerParams(dimension_semantics=("arbitrary",) * 3,
                                             vmem_limit_bytes=VMEM_LIMIT),
        name="nsa",
    )(qn_t, ksel, kwin, vsw, cmp, gates_t, selwt)


def _diff_kernel(qt_ref, k_ref, v_ref, lam_ref, ng_ref, o_ref):
    i = pl.program_id(2)
    tq = ck = Q_TILE_DIFF
    q1 = qt_ref[0:LANE, :]
    q2 = qt_ref[LANE:2 * LANE, :]
    lam_init = 0.8 - 0.6 * math.exp(-0.3 * 0)
    lv = lam_ref[...]
    lam = (jnp.exp(jnp.sum(lv[0:1] * lv[1:2], axis=-1, keepdims=True))
           - jnp.exp(jnp.sum(lv[2:3] * lv[3:4], axis=-1, keepdims=True)) + lam_init)
    def chunk(j):
        start = pl.multiple_of(j * ck, ck)
        return (k_ref[pl.ds(start, ck), 0:LANE], k_ref[pl.ds(start, ck), LANE:2 * LANE],
                v_ref[pl.ds(start, ck), :])

    def body(j, carry):
        c1, c2 = carry
        k1, k2, v = chunk(j)
        return _flash_step(c1, _dot(k1, q1), v), _flash_step(c2, _dot(k2, q2), v)

    init = (_flash_init(tq, DIFF_VDIM), _flash_init(tq, DIFF_VDIM))
    c1, c2 = lax.fori_loop(0, i, body, init)

    hq = tq // 2
    start = pl.multiple_of(i * ck, ck)
    tri = lax.broadcasted_iota(I32, (hq, 1), 0) <= lax.broadcasted_iota(I32, (1, hq), 1)
    lanes = lambda c, sl: tuple(a[:, sl] for a in c)

    def diagonal(c, q, k0):
        ka, va = k_ref[pl.ds(start, hq), k0:k0 + LANE], v_ref[pl.ds(start, hq), :]
        kb, vb = k_ref[pl.ds(start + hq, hq), k0:k0 + LANE], v_ref[pl.ds(start + hq, hq), :]
        early = _flash_step(lanes(c, slice(0, hq)), jnp.where(tri, _dot(ka, q[:, 0:hq]), MASKED), va)
        late = _flash_step(lanes(c, slice(hq, tq)), _dot(ka, q[:, hq:tq]), va)
        late = _flash_step(late, jnp.where(tri, _dot(kb, q[:, hq:tq]), MASKED), vb)
        return jnp.concatenate([_flash_out(early), _flash_out(late)], axis=1)

    o = diagonal(c1, q1, 0) - lam * diagonal(c2, q2, LANE)
    o = o * lax.rsqrt(jnp.mean(o * o, axis=0, keepdims=True) + 1e-5) * ng_ref[...] * (1.0 - lam_init)
    o_ref[...] = o.T.astype(BF)


def _diff(qd_t, kd, vd, lamv, ng_col, batch, seq):
    tq = Q_TILE_DIFF
    nq = seq // tq
    return pl.pallas_call(
        _diff_kernel,
        grid=(batch, DIFF_HEADS, nq),
        in_specs=[pl.BlockSpec((2 * LANE, tq), lambda b, h, i: (h, b * nq + i)),
                  pl.BlockSpec((seq, 2 * LANE), lambda b, h, i: (b, h)),
                  pl.BlockSpec((seq, DIFF_VDIM), lambda b, h, i: (b, h)),
                  pl.BlockSpec((4, DIFF_DH), lambda b, h, i: (0, 0)),
                  pl.BlockSpec((DIFF_VDIM, 1), lambda b, h, i: (0, 0))],
        out_specs=pl.BlockSpec((tq, DIFF_VDIM), lambda b, h, i: (b * nq + i, h)),
        out_shape=jax.ShapeDtypeStruct((batch * seq, DIFF_WIDTH), BF),
        compiler_params=pltpu.CompilerParams(dimension_semantics=("arbitrary",) * 3,
                                             vmem_limit_bytes=VMEM_LIMIT),
        name="diff",
    )(qd_t, kd, vd, lamv, ng_col)


def _merge_kernel(on_ref, od_ref, ga_ref, gb_ref, x_ref, wn_ref, wd_ref, wo_ref, g_ref, b_ref,
                  h_ref, hb_ref):
    merged = (ga_ref[...].astype(F32) * _dot(on_ref[...], wn_ref[...])
              + gb_ref[...].astype(F32) * _dot(od_ref[...], wd_ref[...]))
    mix = _dot(merged.astype(BF), wo_ref[...])
    h = _layer_norm(DEEPNORM_ALPHA * x_ref[...] + mix, g_ref[...], b_ref[...])
    h_ref[...] = h
    hb_ref[...] = h.T.astype(BF)


def _merge(o_nsa, o_diff, ga, gb, x2, wn, wd, wo, g, b, tm=512):
    n = x2.shape[0]
    tok = lambda w: pl.BlockSpec((tm, w), lambda i: (i, 0))
    full = lambda a: pl.BlockSpec(a.shape, lambda i: (0, 0))
    return pl.pallas_call(
        _merge_kernel,
        grid=(n // tm,),
        in_specs=[tok(NSA_WIDTH), tok(DIFF_WIDTH), tok(D_MODEL), tok(D_MODEL), tok(D_MODEL),
                  full(wn), full(wd), full(wo), full(g), full(b)],
        out_specs=[tok(D_MODEL), pl.BlockSpec((D_MODEL, tm), lambda i: (0, i))],
        out_shape=[jax.ShapeDtypeStruct((n, D_MODEL), F32), jax.ShapeDtypeStruct((D_MODEL, n), BF)],
        compiler_params=pltpu.CompilerParams(dimension_semantics=("arbitrary",),
                                             vmem_limit_bytes=VMEM_LIMIT),
        name="merge",
    )(o_nsa, o_diff, ga, gb, x2, wn, wd, wo, g, b)


N_CAND = PEER_TOPK + 8 * (PEER_TOPK - 1)


def _extract_topk(s, k):
    nrow, w = s.shape
    rows = lax.broadcasted_iota(I32, (nrow, w), 0)
    krow = lax.broadcasted_iota(I32, (k, w), 0)
    pos = jnp.full((nrow, w), float(k), F32)
    vals = jnp.zeros((k, w), F32)
    for r in range(k):
        m = jnp.max(s, axis=0, keepdims=True)
        idx = jnp.min(jnp.where(s == m, rows, nrow), axis=0, keepdims=True)
        hit = rows == idx
        pos = jnp.where(hit, float(r), pos)
        s = jnp.where(hit, -jnp.inf, s)
        vals = jnp.where(krow == r, m, vals)
    return vals, pos


def _sort_pairs(n):
    pairs = []

    def merge(lo, hi, r):
        step = r * 2
        if step < hi - lo:
            merge(lo, hi, step)
            merge(lo + r, hi, step)
            pairs.extend((i, i + r) for i in range(lo + r, hi - r, step))
        else:
            pairs.append((lo, lo + r))

    def sort(lo, hi):
        if hi - lo >= 1:
            mid = lo + (hi - lo) // 2
            sort(lo, mid)
            sort(mid + 1, hi)
            merge(lo, hi, 1)

    sort(0, n - 1)
    return pairs


def _sorted_top16(s):
    n = PEER_TOPK
    v = [s[SUBLANES * i:SUBLANES * (i + 1)] for i in range(n)]
    for i, j in _sort_pairs(n):
        v[i], v[j] = jnp.maximum(v[i], v[j]), jnp.minimum(v[i], v[j])
    for shift in (4, 2, 1):
        other = [pltpu.roll(x, shift, 0) for x in v]
        t = [jnp.maximum(v[i], other[n - 1 - i]) for i in range(n)]
        d = n // 2
        while d:
            for i in range(n):
                if not i & d:
                    t[i], t[i + d] = jnp.maximum(t[i], t[i + d]), jnp.minimum(t[i], t[i + d])
            d //= 2
        v = t
    return v


def _route_kernel(h_ref, wq_ref, sk1_ref, sk2_ref, c_ref, p2_ref, e1_ref, e2_ref, q_scr):
    t = h_ref.shape[1]
    q_scr[...] = _dot(wq_ref[...], h_ref[...]).astype(BF)
    half = PEER_DKEY // 2

    crow = lax.broadcasted_iota(I32, (N_CAND, LANE), 0)
    ci = jnp.where(crow < PEER_TOPK, 0, ((crow - PEER_TOPK) >> 3) + 1)
    cj = jnp.where(crow < PEER_TOPK, crow, (crow - PEER_TOPK) & 7)
    cand_ok = (ci + 1) * (cj + 1) <= PEER_TOPK
    krow = lax.broadcasted_iota(I32, (PEER_TOPK, LANE), 0)
    sub = lax.broadcasted_iota(I32, (SUBLANES, LANE), 0)
    frow = lax.broadcasted_iota(I32, (PEER_NKEYS, LANE), 0)
    fi = jnp.maximum((frow >> 3) - 1, 0)
    fj = jnp.where(frow < 2 * SUBLANES, frow, frow & 7)
    cand_fast_ok = ((fi + 1) * (fj + 1) <= PEER_TOPK) | ((frow >> 3 == PEER_TOPK - 1) & (fj == 1))

    def tables(s1, s2):
        v1, pos1 = _extract_topk(s1, PEER_TOPK)
        v2, pos2 = _extract_topk(s2, PEER_TOPK)
        pieces = [v1[0:1] + v2]
        for r in range(1, PEER_TOPK):
            pieces.append(v1[r:r + 1] + v2[0:8])
        cand = jnp.where(cand_ok, jnp.concatenate(pieces, axis=0), -jnp.inf)
        _, cpos = _extract_topk(cand, PEER_TOPK)
        picked = cpos < PEER_TOPK
        top = v1[0:1] + v2[0:1]
        z = jnp.sum(jnp.where(picked, jnp.exp(cand - top), 0.0), axis=0, keepdims=True)
        pk = picked.astype(F32)
        cnt = jnp.zeros((PEER_TOPK, LANE), F32)
        cnt = jnp.where(krow == 0, jnp.sum(pk[0:PEER_TOPK], axis=0, keepdims=True), cnt)
        for r in range(1, PEER_TOPK):
            lo = PEER_TOPK + 8 * (r - 1)
            cnt = jnp.where(krow == r, jnp.sum(pk[lo:lo + 8], axis=0, keepdims=True), cnt)
        c = jnp.zeros((PEER_NKEYS, LANE), F32)
        for r in range(PEER_TOPK):
            c = jnp.where(pos1 == float(r), cnt[r:r + 1], c)
        return c, pos2, jnp.exp(s1 - v1[0:1]) / z, jnp.exp(s2 - v2[0:1])

    def tables_distinct(s1, s2):
        n = PEER_TOPK
        slabs = lambda s: [s[SUBLANES * i:SUBLANES * (i + 1)] for i in range(PEER_NKEYS // SUBLANES)]
        v1 = _sorted_top16(s1)
        v2 = _sorted_top16(s2)
        col = lambda v, base: functools.reduce(
            lambda acc, k: jnp.where(sub == k, v[base + k], acc), range(1, SUBLANES), v[base])
        v2_lo, v2_hi = col(v2, 0), col(v2, SUBLANES)
        last = jnp.where(sub == 1, v1[n - 1] + v2[0], v1[n - 2] + v2_lo)
        pieces = [v1[0] + v2_lo, v1[0] + v2_hi] + [v1[r] + v2_lo for r in range(1, n - 2)] + [last]
        cand = jnp.where(cand_fast_ok, jnp.concatenate(pieces, axis=0), -jnp.inf)
        first = [v1[0], v1[0]] + v1[1:n - 2] + [jnp.where(sub == 1, v1[n - 1], v1[n - 2])]
        top = v1[0] + v2[0]
        vals = _sorted_top16(cand)
        tau = vals[n - 1]
        z = functools.reduce(lambda acc, m: acc + jnp.exp(m[0:1] - top[0:1]), vals[1:],
                             jnp.exp(vals[0][0:1] - top[0:1]))
        count = lambda pred: jnp.sum(jnp.where(pred, 1.0, 0.0), axis=0, keepdims=True)
        ok = count(cand >= tau[0:1]) == n
        for k in range(n - 1):
            ok = (ok & (vals[k][0:1] > vals[k + 1][0:1]) & (v1[k][0:1] > v1[k + 1][0:1])
                  & (v2[k][0:1] > v2[k + 1][0:1]))
        ok = ok & (count(s1 >= v1[n - 1][0:1]) == n) & (count(s2 >= v2[n - 1][0:1]) == n)
        kept = [jnp.where(p >= tau, f, jnp.inf) for p, f in zip(jnp.split(cand, n, axis=0), first)]
        odd = jnp.min(jnp.where(sub == 1, kept[n - 1], jnp.inf), axis=0, keepdims=True)
        kept[n - 1] = jnp.where(sub == 1, jnp.inf, kept[n - 1])
        th_lo = functools.reduce(jnp.minimum, kept[2:], kept[0])
        th_lo = jnp.where(sub == 0, jnp.minimum(th_lo, odd), th_lo)
        th_hi = kept[1]
        theta = ([jnp.broadcast_to(th_lo[j:j + 1], (SUBLANES, LANE)) for j in range(SUBLANES)]
                 + [jnp.broadcast_to(th_hi[j:j + 1], (SUBLANES, LANE)) for j in range(SUBLANES)])
        c, p2 = [], []
        for x1, x2 in zip(slabs(s1), slabs(s2)):
            c.append(functools.reduce(lambda acc, k: jnp.where(x1 >= theta[k], k + 1.0, acc),
                                      range(n), jnp.zeros_like(x1)))
            p2.append(functools.reduce(lambda acc, k: jnp.where(v2[k] > x2, k + 1.0, acc),
                                       range(n), jnp.zeros_like(x2)))
        e1 = jnp.exp(s1 - v1[0][0:1]) / z
        e2 = jnp.exp(s2 - v2[0][0:1])
        return (jnp.concatenate(c, axis=0), jnp.concatenate(p2, axis=0), e1, e2), ok

    def head_body(hd, _):
        base = pl.multiple_of(hd * PEER_DKEY, PEER_DKEY)
        s1_all = _dot(sk1_ref[...], q_scr[pl.ds(base, half), :])
        s2_all = _dot(sk2_ref[...], q_scr[pl.ds(base + half, half), :])
        for lc in range(t // LANE):
            s1 = s1_all[:, lc * LANE:(lc + 1) * LANE]
            s2 = s2_all[:, lc * LANE:(lc + 1) * LANE]

            def store(vals):
                for ref, v in zip((c_ref, p2_ref, e1_ref, e2_ref), vals):
                    ref[hd, lc] = v

            vals, distinct = tables_distinct(s1, s2)
            store(vals)

            @pl.when(jnp.min(jnp.where(distinct, 1.0, 0.0)) < 0.5)
            def _():
                store(tables(s1, s2))
        return 0

    lax.fori_loop(0, PEER_HEADS, head_body, 0)


def _route(hb_t, wq_t, sk1, sk2, tm=256):
    n = hb_t.shape[1]
    full = lambda a: pl.BlockSpec(a.shape, lambda i: (0, 0))
    out_spec = pl.BlockSpec((PEER_HEADS, tm // LANE, PEER_NKEYS, LANE), lambda i: (0, i, 0, 0))
    shape = (PEER_HEADS, n // LANE, PEER_NKEYS, LANE)
    return pl.pallas_call(
        _route_kernel,
        grid=(n // tm,),
        in_specs=[pl.BlockSpec((D_MODEL, tm), lambda i: (0, i)), full(wq_t), full(sk1), full(sk2)],
        out_specs=[out_spec] * 4,
        out_shape=[jax.ShapeDtypeStruct(shape, F32)] * 4,
        scratch_shapes=[pltpu.VMEM((PEER_HEADS * PEER_DKEY, tm), BF)],
        compiler_params=pltpu.CompilerParams(dimension_semantics=("arbitrary",),
                                             vmem_limit_bytes=VMEM_LIMIT),
        name="route",
    )(hb_t, wq_t, sk1, sk2)


def _peer_kernel(x_ref, u_ref, vt_ref, c_ref, p2_ref, e1_ref, e2_ref, o_ref,
                 act0, act1, y0, y1, p2_scr, e2_scr, *, n_chunks):
    s = pl.program_id(1)
    te, tm = y0.shape
    n_a = te // PEER_NKEYS
    groups = PEER_NKEYS // BF_TILE[0]

    @pl.when(s == 0)
    def _():
        o_ref[...] = jnp.zeros_like(o_ref)
        for r in (act0, act1, y0, y1):
            r[...] = jnp.zeros_like(r)
        for hd in range(PEER_HEADS):
            for lc in range(tm // LANE):
                p2_scr[hd, lc] = p2_ref[hd, lc].astype(BF).reshape(groups, *BF_TILE)
                e2_scr[hd, lc] = e2_ref[hd, lc].astype(BF).reshape(groups, *BF_TILE)

    jb = jnp.clip(s - 1, 0, n_chunks - 1)

    def step(act_new, act_old, y_new, y_old):
        o_ref[...] += _dot(vt_ref[0], y_old[...])
        for al in range(n_a):
            a = jb * n_a + al
            for lc in range(tm // LANE):
                w = None
                for hd in range(PEER_HEADS):
                    ca = jnp.broadcast_to(c_ref[hd, lc, pl.ds(a, 1), :], BF_TILE).astype(BF)[None]
                    e1a = jnp.broadcast_to(e1_ref[hd, lc, pl.ds(a, 1), :], BF_TILE).astype(BF)[None]
                    term = jnp.where(p2_scr[hd, lc] < ca, e2_scr[hd, lc], 0.0) * e1a
                    w = term if w is None else w + term
                rows = slice(al * PEER_NKEYS, (al + 1) * PEER_NKEYS)
                cols = slice(lc * LANE, (lc + 1) * LANE)
                y = w * act_old[rows, cols].reshape(groups, *BF_TILE)
                y_new[rows, cols] = y.reshape(PEER_NKEYS, LANE)
        act_new[...] = _gelu(_dot(u_ref[...], x_ref[...])).astype(BF)

    @pl.when(s % 2 == 0)
    def _():
        step(act0, act1, y1, y0)

    @pl.when(s % 2 == 1)
    def _():
        step(act1, act0, y0, y1)


def _peer(hb_t, u_b, vt_b, c, p2, e1, e2, tm=1024):
    n = hb_t.shape[1]
    n_chunks, _, te = vt_b.shape
    last = n_chunks - 1
    rt = pl.BlockSpec((PEER_HEADS, tm // LANE, PEER_NKEYS, LANE), lambda i, s: (0, i, 0, 0),
                      pipeline_mode=pl.Buffered(1))
    packed = pltpu.VMEM((PEER_HEADS, tm // LANE, PEER_NKEYS // BF_TILE[0]) + BF_TILE, BF)
    chunk = pltpu.VMEM((te, tm), BF)
    return pl.pallas_call(
        functools.partial(_peer_kernel, n_chunks=n_chunks),
        grid=(n // tm, n_chunks + 2),
        in_specs=[pl.BlockSpec((D_MODEL, tm), lambda i, s: (0, i)),
                  pl.BlockSpec((te, D_MODEL), lambda i, s: (jnp.minimum(s, last), 0)),
                  pl.BlockSpec((1, D_MODEL, te), lambda i, s: (jnp.clip(s - 2, 0, last), 0, 0)),
                  rt, rt, rt, rt],
        out_specs=pl.BlockSpec((D_MODEL, tm), lambda i, s: (0, i)),
        out_shape=jax.ShapeDtypeStruct((D_MODEL, n), F32),
        scratch_shapes=[chunk, chunk, chunk, chunk, packed, packed],
        compiler_params=pltpu.CompilerParams(dimension_semantics=("arbitrary", "arbitrary"),
                                             vmem_limit_bytes=VMEM_LIMIT),
        name="peer",
    )(hb_t, u_b, vt_b, c, p2, e1, e2)


def _final_kernel(ft_ref, h_ref, p_ref, wg_ref, wp_ref, g_ref, b_ref, o_ref):
    ffn = ft_ref[...].T
    h = _layer_norm(DEEPNORM_ALPHA * h_ref[...] + ffn, g_ref[...], b_ref[...])
    gate = jax.nn.sigmoid(_dot(h.astype(BF), wg_ref[...]))
    o_ref[...] = h + gate * _dot(p_ref[...].astype(BF), wp_ref[...])


def _final(ffn_t, h1, p2d, wg, wp, g, b, tm=512):
    n = h1.shape[0]
    full = lambda a: pl.BlockSpec(a.shape, lambda i: (0, 0))
    return pl.pallas_call(
        _final_kernel,
        grid=(n // tm,),
        in_specs=[pl.BlockSpec((D_MODEL, tm), lambda i: (0, i)),
                  pl.BlockSpec((tm, D_MODEL), lambda i: (i, 0)),
                  pl.BlockSpec((tm, PLE_DIM), lambda i: (i, 0)),
                  full(wg), full(wp), full(g), full(b)],
        out_specs=pl.BlockSpec((tm, D_MODEL), lambda i: (i, 0)),
        out_shape=jax.ShapeDtypeStruct((n, D_MODEL), F32),
        compiler_params=pltpu.CompilerParams(dimension_semantics=("arbitrary",),
                                             vmem_limit_bytes=VMEM_LIMIT),
        name="final",
    )(ffn_t, h1, p2d, wg, wp, g, b)


def _arrange_w_in(w_in):
    cuts = [NSA_WIDTH, 6 * NSA_KV, 3 * NSA_HEADS, 2 * DIFF_QK, 2 * DIFF_QK, DIFF_WIDTH, D_MODEL, D_MODEL]
    offs = [0]
    for c in cuts:
        offs.append(offs[-1] + c)
    w_qn, w_kv, w_g, w_qd, w_kd, w_vd, w_ga, w_gb = [w_in[:, offs[k]:offs[k + 1]] for k in range(8)]
    d = w_in.shape[0]

    def widen(w):
        w = w.reshape(d, -1, HEAD_DIM)
        return jnp.pad(w, ((0, 0), (0, 0), (0, LANE - HEAD_DIM))).reshape(d, -1)

    kv = w_kv.reshape(d, 6, NSA_GROUPS, HEAD_DIM)
    both = lambda a, b: jnp.stack([kv[:, a], kv[:, b]], axis=2).reshape(d, NSA_GROUPS * LANE)
    w_g = w_g.reshape(d, NSA_GROUPS, 3 * NSA_HPG)
    w_g = jnp.pad(w_g, ((0, 0), (0, 0), (0, GATE_ROWS - 3 * NSA_HPG))).reshape(d, NSA_GROUPS * GATE_ROWS)
    pair = lambda w: w.reshape(d, 2, DIFF_HEADS, DIFF_DH).transpose(0, 2, 1, 3).reshape(d, 2 * DIFF_QK)
    cols = {"ksel": widen(kv[:, 2].reshape(d, NSA_KV)), "kwin": widen(kv[:, 4].reshape(d, NSA_KV)),
            "vsw": both(3, 5), "kvc": both(0, 1), "kd": widen(pair(w_kd)), "vd": w_vd, "ga": w_ga, "gb": w_gb}
    rows = {"qn": widen(w_qn * (HEAD_DIM ** -0.5)), "g": w_g,
            "qd": widen(pair(w_qd) * (DIFF_DH ** -0.5))}
    w_all = jnp.concatenate([cols[name] for name, _, _ in PROJ_COLS], axis=1).astype(BF)
    wt_all = jnp.concatenate([rows[name] for name, _, _ in PROJ_ROWS], axis=1).T.astype(BF)
    return w_all, wt_all


def _query_aug():
    slot = jnp.arange(LANE)
    on = ((slot == AUG) | (slot == AUG + 1)).astype(F32)
    nsa = jnp.exp2(-(jnp.arange(NSA_HEADS) + 1.0))
    dif = jnp.repeat(jnp.exp2(-(8.0 / DIFF_HEADS) * (jnp.arange(DIFF_HEADS) + 1.0)), 2)
    parts = {"qn": (nsa[:, None] * on).reshape(-1), "g": jnp.zeros((NSA_GROUPS * GATE_ROWS,), F32),
             "qd": (dif[:, None] * on).reshape(-1)}
    return jnp.concatenate([parts[name] for name, _, _ in PROJ_ROWS]).reshape(PROJ_HEIGHT, 1)


def _sel_weights_t():
    c0 = jnp.arange(N_CMP)[None, :] * CMP_STRIDE
    s0 = jnp.arange(LANE)[:, None] * SEL_BLOCK
    ov = jnp.minimum(c0 + CMP_BLOCK, s0 + SEL_BLOCK) - jnp.maximum(c0, s0)
    w = jnp.clip(ov, 0, None).astype(F32) / CMP_BLOCK
    return jnp.where(jnp.arange(LANE)[:, None] < N_SEL, w, 0.0).astype(BF)


def _token_mixers(x2, batch, seq, w_in, cmp_pos_k, cmp_pos_v, cmp_k_w1, cmp_k_w2, cmp_v_w1, cmp_v_w2,
                  lam_q1, lam_k1, lam_q2, lam_k2, diff_norm_g):
    w_all, wt_all = _arrange_w_in(w_in)
    ksel, kwin, vsw, kvc, kd, vd, ga, gb, qn_t, gates_t, qd_t = _proj(x2, w_all, wt_all, _query_aug(), seq)
    r = kvc.reshape(batch, seq, NSA_GROUPS, 2, HEAD_DIM).transpose(0, 2, 3, 1, 4)
    r = r.reshape(batch, NSA_GROUPS, 2, seq // CMP_STRIDE, CMP_STRIDE * HEAD_DIM)
    pos2 = lambda pe: pe.reshape(2, CMP_STRIDE * HEAD_DIM)
    cmp = _compress(r, pos2(cmp_pos_k), pos2(cmp_pos_v), cmp_k_w1.astype(BF), cmp_k_w2.astype(BF),
                    cmp_v_w1.astype(BF), cmp_v_w2.astype(BF))
    o_nsa = _nsa(qn_t, ksel, kwin, vsw, cmp, gates_t, _sel_weights_t(), batch, seq)
    lamv = jnp.stack([lam_q1, lam_k1, lam_q2, lam_k2]).astype(F32)
    o_diff = _diff(qd_t, kd, vd, lamv, diff_norm_g.reshape(DIFF_VDIM, 1).astype(F32), batch, seq)
    return o_nsa, o_diff, ga, gb


def kernel(x, p, w_in, cmp_pos_k, cmp_pos_v, cmp_k_w1, cmp_k_w2, cmp_v_w1, cmp_v_w2, lam_q1, lam_k1, lam_q2, lam_k2, diff_norm_g, w_branch_nsa, w_branch_diff, w_out, ln1_g, ln1_b, peer_wq, peer_subkeys1, peer_subkeys2, peer_u, peer_v, ln2_g, ln2_b, ple_w_proj, ple_w_gate):
    batch, seq, d = x.shape
    assert seq == N_CMP * CMP_STRIDE and seq == N_SEL * SEL_BLOCK and d == D_MODEL
    x2 = x.reshape(batch * seq, d)
    row = lambda v: v.reshape(1, -1).astype(F32)
    o_nsa, o_diff, ga, gb = _token_mixers(
        x2, batch, seq, w_in[0], cmp_pos_k[0], cmp_pos_v[0], cmp_k_w1[0], cmp_k_w2[0], cmp_v_w1[0],
        cmp_v_w2[0], lam_q1[0], lam_k1[0], lam_q2[0], lam_k2[0], diff_norm_g[0])
    h1, h1b = _merge(o_nsa, o_diff, ga, gb, x2, w_branch_nsa[0].astype(BF), w_branch_diff[0].astype(BF),
                     w_out[0].astype(BF), row(ln1_g[0]), row(ln1_b[0]))
    c, p2, e1, e2 = _route(h1b, peer_wq[0].T.astype(BF), peer_subkeys1[0].astype(BF),
                           peer_subkeys2[0].astype(BF))
    vt = peer_v[0].astype(BF).reshape(PEER_NEXPERTS // PEER_CHUNK, PEER_CHUNK, d).transpose(0, 2, 1)
    ffn_t = _peer(h1b, peer_u[0].astype(BF), vt, c, p2, e1, e2)
    out = _final(ffn_t, h1, p[0].reshape(batch * seq, PLE_DIM), ple_w_gate[0].astype(BF),
                 ple_w_proj[0].astype(BF), row(ln2_g[0]), row(ln2_b[0]))
    return out.reshape(batch, seq, d)
```

```python
import functools
import math

import jax
import jax.numpy as jnp
from jax import lax
from jax.experimental import pallas as pl
from jax.experimental.pallas import tpu as pltpu

D_MODEL = 1024
DEPTH = 1
PLE_DIM = 256

NSA_HEADS = 8
NSA_GROUPS = 2
NSA_HPG = NSA_HEADS // NSA_GROUPS
HEAD_DIM = 64
CMP_BLOCK = 32
CMP_STRIDE = 16
SEL_BLOCK = 64
SEL_TOPK = 8
WINDOW = 512
SEL_FORCE = 1.0e4

DIFF_HEADS = 4
DIFF_DH = 64
DIFF_VDIM = 2 * DIFF_DH

PEER_HEADS = 8
PEER_NKEYS = 128
PEER_NEXPERTS = PEER_NKEYS * PEER_NKEYS
PEER_DKEY = 256
PEER_TOPK = 16

NSA_WIDTH = NSA_HEADS * HEAD_DIM
NSA_KV = NSA_GROUPS * HEAD_DIM
DIFF_QK = DIFF_HEADS * DIFF_DH
DIFF_WIDTH = DIFF_HEADS * DIFF_VDIM

DEEPNORM_ALPHA = (2.0 * DEPTH) ** 0.25
LN_EPS = 1e-5

BF = jnp.bfloat16
F32 = jnp.float32
I32 = jnp.int32

LANE = 128
SUBLANES = 8
BF_TILE = (16, LANE)
Q_TILE_NSA = 512
Q_TILE_DIFF = 1024
N_CMP = 128
N_SEL = 32
PEER_CHUNK = 512
VMEM_LIMIT = 56 * 1024 * 1024

AUG = HEAD_DIM
SEL_LANE0 = AUG + 8
M_INIT = -1.0e30
MASKED = -(2.0 ** 100)
GATE_ROWS = 16

PROJ_COLS = (("ksel", NSA_GROUPS * LANE, BF), ("kwin", NSA_GROUPS * LANE, BF), ("vsw", NSA_GROUPS * LANE, BF),
             ("kvc", NSA_GROUPS * LANE, BF), ("kd", 2 * DIFF_HEADS * LANE, BF), ("vd", DIFF_WIDTH, BF),
             ("ga", D_MODEL, BF), ("gb", D_MODEL, BF))
PROJ_WIDTH = sum(w for _, w, _ in PROJ_COLS)
PROJ_ROWS = (("qn", NSA_HEADS * LANE, BF), ("g", NSA_GROUPS * GATE_ROWS, F32), ("qd", 2 * DIFF_HEADS * LANE, BF))
PROJ_HEIGHT = sum(r for _, r, _ in PROJ_ROWS)


def _dot(a, b):
    return jnp.dot(a, b, preferred_element_type=F32)


def _dot_nt(a, b):
    return lax.dot_general(a, b, (((1,), (1,)), ((), ())), preferred_element_type=F32)


def _dot_tn(a, b):
    return lax.dot_general(a, b, (((0,), (0,)), ((), ())), preferred_element_type=F32)


def _gelu(x):
    return 0.5 * x * (1.0 + lax.erf(x * (2.0 ** -0.5)))


def _layer_norm(y, g, b):
    mu = jnp.mean(y, axis=-1, keepdims=True)
    d = y - mu
    var = jnp.mean(d * d, axis=-1, keepdims=True)
    return d * lax.rsqrt(var + LN_EPS) * g + b


def _split_bf16(v):
    hi = v.astype(BF).astype(F32)
    return hi, v - hi


def _proj_kernel(x_ref, w_ref, wt_ref, qaug_ref, *o_refs, seq):
    i = pl.program_id(0)
    tm = x_ref.shape[0]
    x = x_ref[...].astype(BF)
    t = (i * tm + lax.broadcasted_iota(I32, (tm, LANE), 0)) & (seq - 1)
    lane = lax.broadcasted_iota(I32, (tm, LANE), 1)
    hi, lo = _split_bf16(t.astype(F32))
    k_aug = jnp.where(lane == AUG, hi, jnp.where(lane == AUG + 1, lo, 0.0))
    k_aug_sel = k_aug + jnp.where(lane - SEL_LANE0 == (t >> 6), 1.0, 0.0)
    extra = {"ksel": lambda: jnp.concatenate([k_aug_sel] * NSA_GROUPS, axis=1),
             "kwin": lambda: jnp.concatenate([k_aug] * NSA_GROUPS, axis=1),
             "kd": lambda: jnp.concatenate([k_aug] * (2 * DIFF_HEADS), axis=1)}
    c0 = 0
    for (name, width, dt), o_ref in zip(PROJ_COLS, o_refs):
        v = _dot(x, w_ref[:, c0:c0 + width])
        c0 += width
        if name in extra:
            v = v + extra[name]()
        if name in ("ga", "gb"):
            v = jax.nn.sigmoid(v)
        o_ref[...] = v.astype(dt)
    r0 = 0
    for (name, rows, dt), o_ref in zip(PROJ_ROWS, o_refs[len(PROJ_COLS):]):
        v = _dot_nt(wt_ref[r0:r0 + rows, :], x) + qaug_ref[r0:r0 + rows, :]
        r0 += rows
        if name == "g":
            v = jax.nn.sigmoid(v)
        o_ref[...] = v.astype(dt)


def _proj(x2, w_all, wt_all, qaug, seq, tm=256):
    n = x2.shape[0]
    assert seq & (seq - 1) == 0 and seq % tm == 0
    full = lambda a: pl.BlockSpec(a.shape, lambda i: (0, 0))
    return pl.pallas_call(
        functools.partial(_proj_kernel, seq=seq),
        grid=(n // tm,),
        in_specs=[pl.BlockSpec((tm, D_MODEL), lambda i: (i, 0)), full(w_all), full(wt_all), full(qaug)],
        out_specs=([pl.BlockSpec((tm, w), lambda i: (i, 0)) for _, w, _ in PROJ_COLS]
                   + [pl.BlockSpec((r, tm), lambda i: (0, i)) for _, r, _ in PROJ_ROWS]),
        out_shape=([jax.ShapeDtypeStruct((n, w), dt) for _, w, dt in PROJ_COLS]
                   + [jax.ShapeDtypeStruct((r, n), dt) for _, r, dt in PROJ_ROWS]),
        compiler_params=pltpu.CompilerParams(dimension_semantics=("arbitrary",),
                                             vmem_limit_bytes=VMEM_LIMIT),
        name="proj",
    )(x2, w_all, wt_all, qaug)


def _compress_kernel(r_ref, posk_ref, posv_ref, wk1_ref, wk2_ref, wv1_ref, wv2_ref, o_ref):
    half = CMP_STRIDE * HEAD_DIM
    for idx, (pos_ref, w1_ref, w2_ref) in enumerate(((posk_ref, wk1_ref, wk2_ref),
                                                     (posv_ref, wv1_ref, wv2_ref))):
        r = r_ref[0, 0, idx].astype(F32)
        ra = (r + pos_ref[0:1, :]).astype(BF)
        rb = (r + pos_ref[1:2, :]).astype(BF)
        p1 = _dot(ra, w1_ref[0:half, :])
        p2 = _dot(rb, w1_ref[half:2 * half, :])
        hid = p1 + pltpu.roll(p2, N_CMP - 1, 0)
        c = _dot(_gelu(hid).astype(BF), w2_ref[...])
        o_ref[0, 0, :, idx * LANE:idx * LANE + HEAD_DIM] = c.astype(BF)
    lane = lax.broadcasted_iota(I32, (N_CMP, HEAD_DIM), 1)
    c_end = lax.broadcasted_iota(I32, (N_CMP, HEAD_DIM), 0) * CMP_STRIDE + (CMP_BLOCK - 1)
    hi, lo = _split_bf16(c_end.astype(F32))
    o_ref[0, 0, :, AUG:LANE] = jnp.where(lane == 0, hi, jnp.where(lane == 1, lo, 0.0)).astype(BF)
    o_ref[0, 0, :, LANE + HEAD_DIM:2 * LANE] = jnp.zeros((N_CMP, HEAD_DIM), BF)


def _compress(r, posk, posv, wk1, wk2, wv1, wv2):
    b = r.shape[0]
    full = lambda a: pl.BlockSpec(a.shape, lambda i, j: (0,) * a.ndim)
    return pl.pallas_call(
        _compress_kernel,
        grid=(b, NSA_GROUPS),
        in_specs=[pl.BlockSpec((1, 1, 2, N_CMP, CMP_STRIDE * HEAD_DIM), lambda i, j: (i, j, 0, 0, 0)),
                  full(posk), full(posv), full(wk1), full(wk2), full(wv1), full(wv2)],
        out_specs=pl.BlockSpec((1, 1, N_CMP, 2 * LANE), lambda i, j: (i, j, 0, 0)),
        out_shape=jax.ShapeDtypeStruct((b, NSA_GROUPS, N_CMP, 2 * LANE), BF),
        compiler_params=pltpu.CompilerParams(dimension_semantics=("arbitrary", "arbitrary")),
        name="compress",
    )(r, posk, posv, wk1, wk2, wv1, wv2)


def _flash_step(carry, s, v):
    m, l, acc = carry
    m_new = jnp.maximum(m, jnp.max(s, axis=0, keepdims=True))
    a = jnp.exp(m - m_new)
    p = jnp.exp(s - m_new)
    l = a * l + jnp.sum(p, axis=0, keepdims=True)
    acc = a * acc + _dot_tn(v, p.astype(BF))
    return m_new, l, acc


def _flash_init(rows, dv):
    return (jnp.full((1, rows), M_INIT, F32), jnp.zeros((1, rows), F32), jnp.zeros((dv, rows), F32))


def _flash_out(carry):
    _, l, acc = carry
    return acc / jnp.maximum(l, 1e-30)


def _nsa_kernel(qt_ref, ksel_ref, kwin_ref, vsw_ref, cmp_ref, gt_ref, selwt_ref, o_ref):
    i = pl.program_id(2)
    tq = ck = Q_TILE_NSA
    rows = NSA_HPG * tq
    qt = qt_ref[...]
    q4 = jnp.concatenate([qt[h * LANE:(h + 1) * LANE, :] for h in range(NSA_HPG)], axis=1)
    t_lane = i * tq + (lax.broadcasted_iota(I32, (1, rows), 1) & (tq - 1))
    key_col = lax.broadcasted_iota(I32, (ck, 1), 0)
    causal = (i * ck + key_col) <= t_lane

    kc = cmp_ref[0, 0, :, 0:LANE]
    vc = cmp_ref[0, 0, :, LANE:LANE + HEAD_DIM]
    c_end = lax.broadcasted_iota(I32, (N_CMP, 1), 0) * CMP_STRIDE + (CMP_BLOCK - 1)
    s = jnp.where(c_end <= t_lane, _dot(kc, q4), MASKED)
    m = jnp.maximum(jnp.max(s, axis=0, keepdims=True), M_INIT)
    e = jnp.exp(s - m)
    p_cmp = e / jnp.maximum(jnp.sum(e, axis=0, keepdims=True), 1e-30)
    o_cmp = _dot_tn(vc, p_cmp.astype(BF))

    psum = p_cmp[:, 0:tq] + p_cmp[:, tq:2 * tq] + p_cmp[:, 2 * tq:3 * tq] + p_cmp[:, 3 * tq:4 * tq]
    p_hi = psum.astype(BF)
    p_lo = (psum - p_hi.astype(F32)).astype(BF)
    selwt = selwt_ref[...]
    imp = (_dot(selwt, p_hi) + _dot(selwt, p_lo))[0:N_SEL]
    jb = lax.broadcasted_iota(I32, (N_SEL, tq), 0)
    tq_pos = i * tq + lax.broadcasted_iota(I32, (N_SEL, tq), 1)
    cur = tq_pos >> 6
    forced = (jb == 0) | (jb == cur) | (jb == cur - 1)
    score = jnp.where(forced, SEL_FORCE, jnp.where(jb * SEL_BLOCK <= tq_pos, imp, -SEL_FORCE))
    unsel = jnp.full((N_SEL, tq), MASKED, F32)
    for _ in range(SEL_TOPK):
        mx = jnp.max(score, axis=0, keepdims=True)
        idx = jnp.min(jnp.where(score == mx, jb, N_SEL), axis=0, keepdims=True)
        hit = jb == idx
        unsel = jnp.where(hit, 0.0, unsel)
        score = jnp.where(hit, -jnp.inf, score)
    neg = jnp.concatenate([jnp.zeros((SEL_LANE0, tq), F32), unsel,
                           jnp.zeros((LANE - SEL_LANE0 - N_SEL, tq), F32)], axis=0)
    q4s = (q4.astype(F32) + jnp.concatenate([neg] * NSA_HPG, axis=1)).astype(BF)

    def chunk(ref, j, c0, width):
        start = pl.multiple_of(j * ck, ck)
        return ref[pl.ds(start, ck), c0:c0 + width]

    def sel_body(j, carry):
        return _flash_step(carry, _dot(chunk(ksel_ref, j, 0, LANE), q4s), chunk(vsw_ref, j, 0, HEAD_DIM))

    carry = lax.fori_loop(0, i, sel_body, _flash_init(rows, HEAD_DIM))
    s = jnp.where(causal, _dot(chunk(ksel_ref, i, 0, LANE), q4s), MASKED)
    o_slc = _flash_out(_flash_step(carry, s, chunk(vsw_ref, i, 0, HEAD_DIM)))

    carry = _flash_init(rows, HEAD_DIM)
    n_back = WINDOW // ck
    for back in range(n_back, 0, -1):
        j = jnp.maximum(i - back, 0)
        ok = i >= back
        if back == n_back:
            ok = ok & (t_lane - (j * ck + key_col) < WINDOW)
        s = jnp.where(ok, _dot(chunk(kwin_ref, j, 0, LANE), q4), MASKED)
        carry = _flash_step(carry, s, chunk(vsw_ref, j, HEAD_DIM, HEAD_DIM))
    s = jnp.where(causal, _dot(chunk(kwin_ref, i, 0, LANE), q4), MASKED)
    o_win = _flash_out(_flash_step(carry, s, chunk(vsw_ref, i, HEAD_DIM, HEAD_DIM)))

    gates = gt_ref[...]
    outs = []
    for h in range(NSA_HPG):
        c0 = h * tq
        outs.append(gates[3 * h:3 * h + 1] * o_cmp[:, c0:c0 + tq]
                    + gates[3 * h + 1:3 * h + 2] * o_slc[:, c0:c0 + tq]
                    + gates[3 * h + 2:3 * h + 3] * o_win[:, c0:c0 + tq])
    o_ref[...] = jnp.concatenate(outs, axis=0).T.astype(BF)


def _nsa(qn_t, ksel, kwin, vsw, cmp, gates_t, selwt, batch, seq):
    tq = Q_TILE_NSA
    assert WINDOW % tq == 0 and seq % tq == 0
    nq = seq // tq
    per_seq = pl.BlockSpec((seq, LANE), lambda b, g, i: (b, g))
    return pl.pallas_call(
        _nsa_kernel,
        grid=(batch, NSA_GROUPS, nq),
        in_specs=[pl.BlockSpec((NSA_HPG * LANE, tq), lambda b, g, i: (g, b * nq + i)),
                  per_seq, per_seq, per_seq,
                  pl.BlockSpec((1, 1, N_CMP, 2 * LANE), lambda b, g, i: (b, g, 0, 0)),
                  pl.BlockSpec((GATE_ROWS, tq), lambda b, g, i: (g, b * nq + i)),
                  pl.BlockSpec((LANE, LANE), lambda b, g, i: (0, 0))],
        out_specs=pl.BlockSpec((tq, NSA_HPG * HEAD_DIM), lambda b, g, i: (b * nq + i, g)),
        out_shape=jax.ShapeDtypeStruct((batch * seq, NSA_WIDTH), BF),
        compiler_params=pltpu.CompilerParams(dimension_semantics=("arbitrary",) * 3,
                                             vmem_limit_bytes=VMEM_LIMIT),
        name="nsa",
    )(qn_t, ksel, kwin, vsw, cmp, gates_t, selwt)


def _diff_kernel(qt_ref, k_ref, v_ref, lam_ref, ng_ref, o_ref):
    i = pl.program_id(2)
    tq = ck = Q_TILE_DIFF
    q1 = qt_ref[0:LANE, :]
    q2 = qt_ref[LANE:2 * LANE, :]
    lam_init = 0.8 - 0.6 * math.exp(-0.3 * 0)
    lv = lam_ref[...]
    lam = (jnp.exp(jnp.sum(lv[0:1] * lv[1:2], axis=-1, keepdims=True))
           - jnp.exp(jnp.sum(lv[2:3] * lv[3:4], axis=-1, keepdims=True)) + lam_init)
    def chunk(j):
        start = pl.multiple_of(j * ck, ck)
        return (k_ref[pl.ds(start, ck), 0:LANE], k_ref[pl.ds(start, ck), LANE:2 * LANE],
                v_ref[pl.ds(start, ck), :])

    def body(j, carry):
        c1, c2 = carry
        k1, k2, v = chunk(j)
        return _flash_step(c1, _dot(k1, q1), v), _flash_step(c2, _dot(k2, q2), v)

    init = (_flash_init(tq, DIFF_VDIM), _flash_init(tq, DIFF_VDIM))
    c1, c2 = lax.fori_loop(0, i, body, init)

    hq = tq // 2
    start = pl.multiple_of(i * ck, ck)
    tri = lax.broadcasted_iota(I32, (hq, 1), 0) <= lax.broadcasted_iota(I32, (1, hq), 1)
    lanes = lambda c, sl: tuple(a[:, sl] for a in c)

    def diagonal(c, q, k0):
        ka, va = k_ref[pl.ds(start, hq), k0:k0 + LANE], v_ref[pl.ds(start, hq), :]
        kb, vb = k_ref[pl.ds(start + hq, hq), k0:k0 + LANE], v_ref[pl.ds(start + hq, hq), :]
        early = _flash_step(lanes(c, slice(0, hq)), jnp.where(tri, _dot(ka, q[:, 0:hq]), MASKED), va)
        late = _flash_step(lanes(c, slice(hq, tq)), _dot(ka, q[:, hq:tq]), va)
        late = _flash_step(late, jnp.where(tri, _dot(kb, q[:, hq:tq]), MASKED), vb)
        return jnp.concatenate([_flash_out(early), _flash_out(late)], axis=1)

    o = diagonal(c1, q1, 0) - lam * diagonal(c2, q2, LANE)
    o = o * lax.rsqrt(jnp.mean(o * o, axis=0, keepdims=True) + 1e-5) * ng_ref[...] * (1.0 - lam_init)
    o_ref[...] = o.T.astype(BF)


def _diff(qd_t, kd, vd, lamv, ng_col, batch, seq):
    tq = Q_TILE_DIFF
    nq = seq // tq
    return pl.pallas_call(
        _diff_kernel,
        grid=(batch, DIFF_HEADS, nq),
        in_specs=[pl.BlockSpec((2 * LANE, tq), lambda b, h, i: (h, b * nq + i)),
                  pl.BlockSpec((seq, 2 * LANE), lambda b, h, i: (b, h)),
                  pl.BlockSpec((seq, DIFF_VDIM), lambda b, h, i: (b, h)),
                  pl.BlockSpec((4, DIFF_DH), lambda b, h, i: (0, 0)),
                  pl.BlockSpec((DIFF_VDIM, 1), lambda b, h, i: (0, 0))],
        out_specs=pl.BlockSpec((tq, DIFF_VDIM), lambda b, h, i: (b * nq + i, h)),
        out_shape=jax.ShapeDtypeStruct((batch * seq, DIFF_WIDTH), BF),
        compiler_params=pltpu.CompilerParams(dimension_semantics=("arbitrary",) * 3,
                                             vmem_limit_bytes=VMEM_LIMIT),
        name="diff",
    )(qd_t, kd, vd, lamv, ng_col)


def _merge_kernel(on_ref, od_ref, ga_ref, gb_ref, x_ref, wn_ref, wd_ref, wo_ref, g_ref, b_ref,
                  h_ref, hb_ref):
    merged = (ga_ref[...].astype(F32) * _dot(on_ref[...], wn_ref[...])
              + gb_ref[...].astype(F32) * _dot(od_ref[...], wd_ref[...]))
    mix = _dot(merged.astype(BF), wo_ref[...])
    h = _layer_norm(DEEPNORM_ALPHA * x_ref[...] + mix, g_ref[...], b_ref[...])
    h_ref[...] = h
    hb_ref[...] = h.T.astype(BF)


def _merge(o_nsa, o_diff, ga, gb, x2, wn, wd, wo, g, b, tm=512):
    n = x2.shape[0]
    tok = lambda w: pl.BlockSpec((tm, w), lambda i: (i, 0))
    full = lambda a: pl.BlockSpec(a.shape, lambda i: (0, 0))
    return pl.pallas_call(
        _merge_kernel,
        grid=(n // tm,),
        in_specs=[tok(NSA_WIDTH), tok(DIFF_WIDTH), tok(D_MODEL), tok(D_MODEL), tok(D_MODEL),
                  full(wn), full(wd), full(wo), full(g), full(b)],
        out_specs=[tok(D_MODEL), pl.BlockSpec((D_MODEL, tm), lambda i: (0, i))],
        out_shape=[jax.ShapeDtypeStruct((n, D_MODEL), F32), jax.ShapeDtypeStruct((D_MODEL, n), BF)],
        compiler_params=pltpu.CompilerParams(dimension_semantics=("arbitrary",),
                                             vmem_limit_bytes=VMEM_LIMIT),
        name="merge",
    )(o_nsa, o_diff, ga, gb, x2, wn, wd, wo, g, b)


N_CAND = PEER_TOPK + 8 * (PEER_TOPK - 1)


def _extract_topk(s, k):
    nrow, w = s.shape
    rows = lax.broadcasted_iota(I32, (nrow, w), 0)
    krow = lax.broadcasted_iota(I32, (k, w), 0)
    pos = jnp.full((nrow, w), float(k), F32)
    vals = jnp.zeros((k, w), F32)
    for r in range(k):
        m = jnp.max(s, axis=0, keepdims=True)
        idx = jnp.min(jnp.where(s == m, rows, nrow), axis=0, keepdims=True)
        hit = rows == idx
        pos = jnp.where(hit, float(r), pos)
        s = jnp.where(hit, -jnp.inf, s)
        vals = jnp.where(krow == r, m, vals)
    return vals, pos


def _sort_pairs(n):
    pairs = []

    def merge(lo, hi, r):
        step = r * 2
        if step < hi - lo:
            merge(lo, hi, step)
            merge(lo + r, hi, step)
            pairs.extend((i, i + r) for i in range(lo + r, hi - r, step))
        else:
            pairs.append((lo, lo + r))

    def sort(lo, hi):
        if hi - lo >= 1:
            mid = lo + (hi - lo) // 2
            sort(lo, mid)
            sort(mid + 1, hi)
            merge(lo, hi, 1)

    sort(0, n - 1)
    return pairs


def _sorted_top16(s):
    n = PEER_TOPK
    v = [s[SUBLANES * i:SUBLANES * (i + 1)] for i in range(n)]
    for i, j in _sort_pairs(n):
        v[i], v[j] = jnp.maximum(v[i], v[j]), jnp.minimum(v[i], v[j])
    for shift in (4, 2, 1):
        other = [pltpu.roll(x, shift, 0) for x in v]
        t = [jnp.maximum(v[i], other[n - 1 - i]) for i in range(n)]
        d = n // 2
        while d:
            for i in range(n):
                if not i & d:
                    t[i], t[i + d] = jnp.maximum(t[i], t[i + d]), jnp.minimum(t[i], t[i + d])
            d //= 2
        v = t
    return v


def _route_kernel(h_ref, wq_ref, sk1_ref, sk2_ref, c_ref, p2_ref, e1_ref, e2_ref, q_scr):
    t = h_ref.shape[1]
    q_scr[...] = _dot(wq_ref[...], h_ref[...]).astype(BF)
    half = PEER_DKEY // 2

    crow = lax.broadcasted_iota(I32, (N_CAND, LANE), 0)
    ci = jnp.where(crow < PEER_TOPK, 0, ((crow - PEER_TOPK) >> 3) + 1)
    cj = jnp.where(crow < PEER_TOPK, crow, (crow - PEER_TOPK) & 7)
    cand_ok = (ci + 1) * (cj + 1) <= PEER_TOPK
    krow = lax.broadcasted_iota(I32, (PEER_TOPK, LANE), 0)
    sub = lax.broadcasted_iota(I32, (SUBLANES, LANE), 0)
    frow = lax.broadcasted_iota(I32, (PEER_NKEYS, LANE), 0)
    fi = jnp.maximum((frow >> 3) - 1, 0)
    fj = jnp.where(frow < 2 * SUBLANES, frow, frow & 7)
    cand_fast_ok = ((fi + 1) * (fj + 1) <= PEER_TOPK) | ((frow >> 3 == PEER_TOPK - 1) & (fj == 1))

    def tables(s1, s2):
        v1, pos1 = _extract_topk(s1, PEER_TOPK)
        v2, pos2 = _extract_topk(s2, PEER_TOPK)
        pieces = [v1[0:1] + v2]
        for r in range(1, PEER_TOPK):
            pieces.append(v1[r:r + 1] + v2[0:8])
        cand = jnp.where(cand_ok, jnp.concatenate(pieces, axis=0), -jnp.inf)
        _, cpos = _extract_topk(cand, PEER_TOPK)
        picked = cpos < PEER_TOPK
        top = v1[0:1] + v2[0:1]
        z = jnp.sum(jnp.where(picked, jnp.exp(cand - top), 0.0), axis=0, keepdims=True)
        pk = picked.astype(F32)
        cnt = jnp.zeros((PEER_TOPK, LANE), F32)
        cnt = jnp.where(krow == 0, jnp.sum(pk[0:PEER_TOPK], axis=0, keepdims=True), cnt)
        for r in range(1, PEER_TOPK):
            lo = PEER_TOPK + 8 * (r - 1)
            cnt = jnp.where(krow == r, jnp.sum(pk[lo:lo + 8], axis=0, keepdims=True), cnt)
        c = jnp.zeros((PEER_NKEYS, LANE), F32)
        for r in range(PEER_TOPK):
            c = jnp.where(pos1 == float(r), cnt[r:r + 1], c)
        return c, pos2, jnp.exp(s1 - v1[0:1]) / z, jnp.exp(s2 - v2[0:1])

    def tables_distinct(s1, s2):
        n = PEER_TOPK
        slabs = lambda s: [s[SUBLANES * i:SUBLANES * (i + 1)] for i in range(PEER_NKEYS // SUBLANES)]
        v1 = _sorted_top16(s1)
        v2 = _sorted_top16(s2)
        col = lambda v, base: functools.reduce(
            lambda acc, k: jnp.where(sub == k, v[base + k], acc), range(1, SUBLANES), v[base])
        v2_lo, v2_hi = col(v2, 0), col(v2, SUBLANES)
        last = jnp.where(sub == 1, v1[n - 1] + v2[0], v1[n - 2] + v2_lo)
        pieces = [v1[0] + v2_lo, v1[0] + v2_hi] + [v1[r] + v2_lo for r in range(1, n - 2)] + [last]
        cand = jnp.where(cand_fast_ok, jnp.concatenate(pieces, axis=0), -jnp.inf)
        first = [v1[0], v1[0]] + v1[1:n - 2] + [jnp.where(sub == 1, v1[n - 1], v1[n - 2])]
        top = v1[0] + v2[0]
        vals = _sorted_top16(cand)
        tau = vals[n - 1]
        z = functools.reduce(lambda acc, m: acc + jnp.exp(m[0:1] - top[0:1]), vals[1:],
                             jnp.exp(vals[0][0:1] - top[0:1]))
        count = lambda pred: jnp.sum(jnp.where(pred, 1.0, 0.0), axis=0, keepdims=True)
        ok = count(cand >= tau[0:1]) == n
        for k in range(n - 1):
            ok = (ok & (vals[k][0:1] > vals[k + 1][0:1]) & (v1[k][0:1] > v1[k + 1][0:1])
                  & (v2[k][0:1] > v2[k + 1][0:1]))
        ok = ok & (count(s1 >= v1[n - 1][0:1]) == n) & (count(s2 >= v2[n - 1][0:1]) == n)
        kept = [jnp.where(p >= tau, f, jnp.inf) for p, f in zip(jnp.split(cand, n, axis=0), first)]
        odd = jnp.min(jnp.where(sub == 1, kept[n - 1], jnp.inf), axis=0, keepdims=True)
        kept[n - 1] = jnp.where(sub == 1, jnp.inf, kept[n - 1])
        th_lo = functools.reduce(jnp.minimum, kept[2:], kept[0])
        th_lo = jnp.where(sub == 0, jnp.minimum(th_lo, odd), th_lo)
        th_hi = kept[1]
        theta = ([jnp.broadcast_to(th_lo[j:j + 1], (SUBLANES, LANE)) for j in range(SUBLANES)]
                 + [jnp.broadcast_to(th_hi[j:j + 1], (SUBLANES, LANE)) for j in range(SUBLANES)])
        c, p2 = [], []
        for x1, x2 in zip(slabs(s1), slabs(s2)):
            c.append(functools.reduce(lambda acc, k: jnp.where(x1 >= theta[k], k + 1.0, acc),
                                      range(n), jnp.zeros_like(x1)))
            p2.append(functools.reduce(lambda acc, k: jnp.where(v2[k] > x2, k + 1.0, acc),
                                       range(n), jnp.zeros_like(x2)))
        e1 = jnp.exp(s1 - v1[0][0:1]) / z
        e2 = jnp.exp(s2 - v2[0][0:1])
        return (jnp.concatenate(c, axis=0), jnp.concatenate(p2, axis=0), e1, e2), ok

    def head_body(hd, _):
        base = pl.multiple_of(hd * PEER_DKEY, PEER_DKEY)
        s1_all = _dot(sk1_ref[...], q_scr[pl.ds(base, half), :])
        s2_all = _dot(sk2_ref[...], q_scr[pl.ds(base + half, half), :])
        for lc in range(t // LANE):
            s1 = s1_all[:, lc * LANE:(lc + 1) * LANE]
            s2 = s2_all[:, lc * LANE:(lc + 1) * LANE]

            def store(vals):
                for ref, v in zip((c_ref, p2_ref, e1_ref, e2_ref), vals):
                    ref[hd, lc] = v

            vals, distinct = tables_distinct(s1, s2)
            store(vals)

            @pl.when(jnp.min(jnp.where(distinct, 1.0, 0.0)) < 0.5)
            def _():
                store(tables(s1, s2))
        return 0

    lax.fori_loop(0, PEER_HEADS, head_body, 0)


def _route(hb_t, wq_t, sk1, sk2, tm=256):
    n = hb_t.shape[1]
    full = lambda a: pl.BlockSpec(a.shape, lambda i: (0, 0))
    out_spec = pl.BlockSpec((PEER_HEADS, tm // LANE, PEER_NKEYS, LANE), lambda i: (0, i, 0, 0))
    shape = (PEER_HEADS, n // LANE, PEER_NKEYS, LANE)
    return pl.pallas_call(
        _route_kernel,
        grid=(n // tm,),
        in_specs=[pl.BlockSpec((D_MODEL, tm), lambda i: (0, i)), full(wq_t), full(sk1), full(sk2)],
        out_specs=[out_spec] * 4,
        out_shape=[jax.ShapeDtypeStruct(shape, F32)] * 4,
        scratch_shapes=[pltpu.VMEM((PEER_HEADS * PEER_DKEY, tm), BF)],
        compiler_params=pltpu.CompilerParams(dimension_semantics=("arbitrary",),
                                             vmem_limit_bytes=VMEM_LIMIT),
        name="route",
    )(hb_t, wq_t, sk1, sk2)


def _peer_kernel(x_ref, u_ref, vt_ref, c_ref, p2_ref, e1_ref, e2_ref, o_ref,
                 act0, act1, y0, y1, p2_scr, e2_scr, *, n_chunks):
    s = pl.program_id(1)
    te, tm = y0.shape
    n_a = te // PEER_NKEYS
    groups = PEER_NKEYS // BF_TILE[0]

    @pl.when(s == 0)
    def _():
        o_ref[...] = jnp.zeros_like(o_ref)
        for r in (act0, act1, y0, y1):
            r[...] = jnp.zeros_like(r)
        for hd in range(PEER_HEADS):
            for lc in range(tm // LANE):
                p2_scr[hd, lc] = p2_ref[hd, lc].astype(BF).reshape(groups, *BF_TILE)
                e2_scr[hd, lc] = e2_ref[hd, lc].astype(BF).reshape(groups, *BF_TILE)

    jb = jnp.clip(s - 1, 0, n_chunks - 1)

    def step(act_new, act_old, y_new, y_old):
        o_ref[...] += _dot_tn(vt_ref[...], y_old[...])
        for al in range(n_a):
            a = jb * n_a + al
            for lc in range(tm // LANE):
                w = None
                for hd in range(PEER_HEADS):
                    ca = jnp.broadcast_to(c_ref[hd, lc, pl.ds(a, 1), :], BF_TILE).astype(BF)[None]
                    e1a = jnp.broadcast_to(e1_ref[hd, lc, pl.ds(a, 1), :], BF_TILE).astype(BF)[None]
                    term = jnp.where(p2_scr[hd, lc] < ca, e2_scr[hd, lc], 0.0) * e1a
                    w = term if w is None else w + term
                rows = slice(al * PEER_NKEYS, (al + 1) * PEER_NKEYS)
                cols = slice(lc * LANE, (lc + 1) * LANE)
                y = w * act_old[rows, cols].reshape(groups, *BF_TILE)
                y_new[rows, cols] = y.reshape(PEER_NKEYS, LANE)
        act_new[...] = _gelu(_dot(u_ref[...], x_ref[...])).astype(BF)

    @pl.when(s % 2 == 0)
    def _():
        step(act0, act1, y1, y0)

    @pl.when(s % 2 == 1)
    def _():
        step(act1, act0, y0, y1)


def _peer(hb_t, u_b, vt_b, c, p2, e1, e2, tm=1024):
    n = hb_t.shape[1]
    te = PEER_CHUNK
    n_chunks = vt_b.shape[0] // te
    last = n_chunks - 1
    rt = pl.BlockSpec((PEER_HEADS, tm // LANE, PEER_NKEYS, LANE), lambda i, s: (0, i, 0, 0),
                      pipeline_mode=pl.Buffered(1))
    packed = pltpu.VMEM((PEER_HEADS, tm // LANE, PEER_NKEYS // BF_TILE[0]) + BF_TILE, BF)
    chunk = pltpu.VMEM((te, tm), BF)
    return pl.pallas_call(
        functools.partial(_peer_kernel, n_chunks=n_chunks),
        grid=(n // tm, n_chunks + 2),
        in_specs=[pl.BlockSpec((D_MODEL, tm), lambda i, s: (0, i)),
                  pl.BlockSpec((te, D_MODEL), lambda i, s: (jnp.minimum(s, last), 0)),
                  pl.BlockSpec((te, D_MODEL), lambda i, s: (jnp.clip(s - 2, 0, last), 0)),
                  rt, rt, rt, rt],
        out_specs=pl.BlockSpec((D_MODEL, tm), lambda i, s: (0, i)),
        out_shape=jax.ShapeDtypeStruct((D_MODEL, n), F32),
        scratch_shapes=[chunk, chunk, chunk, chunk, packed, packed],
        compiler_params=pltpu.CompilerParams(dimension_semantics=("arbitrary", "arbitrary"),
                                             vmem_limit_bytes=VMEM_LIMIT),
        name="peer",
    )(hb_t, u_b, vt_b, c, p2, e1, e2)


def _final_kernel(ft_ref, h_ref, p_ref, wg_ref, wp_ref, g_ref, b_ref, o_ref):
    ffn = ft_ref[...].T
    h = _layer_norm(DEEPNORM_ALPHA * h_ref[...] + ffn, g_ref[...], b_ref[...])
    gate = jax.nn.sigmoid(_dot(h.astype(BF), wg_ref[...]))
    o_ref[...] = h + gate * _dot(p_ref[...].astype(BF), wp_ref[...])


def _final(ffn_t, h1, p2d, wg, wp, g, b, tm=512):
    n = h1.shape[0]
    full = lambda a: pl.BlockSpec(a.shape, lambda i: (0, 0))
    return pl.pallas_call(
        _final_kernel,
        grid=(n // tm,),
        in_specs=[pl.BlockSpec((D_MODEL, tm), lambda i: (0, i)),
                  pl.BlockSpec((tm, D_MODEL), lambda i: (i, 0)),
                  pl.BlockSpec((tm, PLE_DIM), lambda i: (i, 0)),
                  full(wg), full(wp), full(g), full(b)],
        out_specs=pl.BlockSpec((tm, D_MODEL), lambda i: (i, 0)),
        out_shape=jax.ShapeDtypeStruct((n, D_MODEL), F32),
        compiler_params=pltpu.CompilerParams(dimension_semantics=("arbitrary",),
                                             vmem_limit_bytes=VMEM_LIMIT),
        name="final",
    )(ffn_t, h1, p2d, wg, wp, g, b)


def _arrange_w_in(w_in):
    cuts = [NSA_WIDTH, 6 * NSA_KV, 3 * NSA_HEADS, 2 * DIFF_QK, 2 * DIFF_QK, DIFF_WIDTH, D_MODEL, D_MODEL]
    offs = [0]
    for c in cuts:
        offs.append(offs[-1] + c)
    w_qn, w_kv, w_g, w_qd, w_kd, w_vd, w_ga, w_gb = [w_in[:, offs[k]:offs[k + 1]] for k in range(8)]
    d = w_in.shape[0]

    def widen(w):
        w = w.reshape(d, -1, HEAD_DIM)
        return jnp.pad(w, ((0, 0), (0, 0), (0, LANE - HEAD_DIM))).reshape(d, -1)

    kv = w_kv.reshape(d, 6, NSA_GROUPS, HEAD_DIM)
    both = lambda a, b: jnp.stack([kv[:, a], kv[:, b]], axis=2).reshape(d, NSA_GROUPS * LANE)
    w_g = w_g.reshape(d, NSA_GROUPS, 3 * NSA_HPG)
    w_g = jnp.pad(w_g, ((0, 0), (0, 0), (0, GATE_ROWS - 3 * NSA_HPG))).reshape(d, NSA_GROUPS * GATE_ROWS)
    pair = lambda w: w.reshape(d, 2, DIFF_HEADS, DIFF_DH).transpose(0, 2, 1, 3).reshape(d, 2 * DIFF_QK)
    cols = {"ksel": widen(kv[:, 2].reshape(d, NSA_KV)), "kwin": widen(kv[:, 4].reshape(d, NSA_KV)),
            "vsw": both(3, 5), "kvc": both(0, 1), "kd": widen(pair(w_kd)), "vd": w_vd, "ga": w_ga, "gb": w_gb}
    rows = {"qn": widen(w_qn * (HEAD_DIM ** -0.5)), "g": w_g,
            "qd": widen(pair(w_qd) * (DIFF_DH ** -0.5))}
    w_all = jnp.concatenate([cols[name] for name, _, _ in PROJ_COLS], axis=1).astype(BF)
    wt_all = jnp.concatenate([rows[name] for name, _, _ in PROJ_ROWS], axis=1).T.astype(BF)
    return w_all, wt_all


def _query_aug():
    slot = jnp.arange(LANE)
    on = ((slot == AUG) | (slot == AUG + 1)).astype(F32)
    nsa = jnp.exp2(-(jnp.arange(NSA_HEADS) + 1.0))
    dif = jnp.repeat(jnp.exp2(-(8.0 / DIFF_HEADS) * (jnp.arange(DIFF_HEADS) + 1.0)), 2)
    parts = {"qn": (nsa[:, None] * on).reshape(-1), "g": jnp.zeros((NSA_GROUPS * GATE_ROWS,), F32),
             "qd": (dif[:, None] * on).reshape(-1)}
    return jnp.concatenate([parts[name] for name, _, _ in PROJ_ROWS]).reshape(PROJ_HEIGHT, 1)


def _sel_weights_t():
    c0 = jnp.arange(N_CMP)[None, :] * CMP_STRIDE
    s0 = jnp.arange(LANE)[:, None] * SEL_BLOCK
    ov = jnp.minimum(c0 + CMP_BLOCK, s0 + SEL_BLOCK) - jnp.maximum(c0, s0)
    w = jnp.clip(ov, 0, None).astype(F32) / CMP_BLOCK
    return jnp.where(jnp.arange(LANE)[:, None] < N_SEL, w, 0.0).astype(BF)


def _token_mixers(x2, batch, seq, w_in, cmp_pos_k, cmp_pos_v, cmp_k_w1, cmp_k_w2, cmp_v_w1, cmp_v_w2,
                  lam_q1, lam_k1, lam_q2, lam_k2, diff_norm_g):
    w_all, wt_all = _arrange_w_in(w_in)
    ksel, kwin, vsw, kvc, kd, vd, ga, gb, qn_t, gates_t, qd_t = _proj(x2, w_all, wt_all, _query_aug(), seq)
    r = kvc.reshape(batch, seq, NSA_GROUPS, 2, HEAD_DIM).transpose(0, 2, 3, 1, 4)
    r = r.reshape(batch, NSA_GROUPS, 2, seq // CMP_STRIDE, CMP_STRIDE * HEAD_DIM)
    pos2 = lambda pe: pe.reshape(2, CMP_STRIDE * HEAD_DIM)
    cmp = _compress(r, pos2(cmp_pos_k), pos2(cmp_pos_v), cmp_k_w1.astype(BF), cmp_k_w2.astype(BF),
                    cmp_v_w1.astype(BF), cmp_v_w2.astype(BF))
    o_nsa = _nsa(qn_t, ksel, kwin, vsw, cmp, gates_t, _sel_weights_t(), batch, seq)
    lamv = jnp.stack([lam_q1, lam_k1, lam_q2, lam_k2]).astype(F32)
    o_diff = _diff(qd_t, kd, vd, lamv, diff_norm_g.reshape(DIFF_VDIM, 1).astype(F32), batch, seq)
    return o_nsa, o_diff, ga, gb


def kernel(x, p, w_in, cmp_pos_k, cmp_pos_v, cmp_k_w1, cmp_k_w2, cmp_v_w1, cmp_v_w2, lam_q1, lam_k1, lam_q2, lam_k2, diff_norm_g, w_branch_nsa, w_branch_diff, w_out, ln1_g, ln1_b, peer_wq, peer_subkeys1, peer_subkeys2, peer_u, peer_v, ln2_g, ln2_b, ple_w_proj, ple_w_gate):
    batch, seq, d = x.shape
    assert seq == N_CMP * CMP_STRIDE and seq == N_SEL * SEL_BLOCK and d == D_MODEL
    x2 = x.reshape(batch * seq, d)
    row = lambda v: v.reshape(1, -1).astype(F32)
    o_nsa, o_diff, ga, gb = _token_mixers(
        x2, batch, seq, w_in[0], cmp_pos_k[0], cmp_pos_v[0], cmp_k_w1[0], cmp_k_w2[0], cmp_v_w1[0],
        cmp_v_w2[0], lam_q1[0], lam_k1[0], lam_q2[0], lam_k2[0], diff_norm_g[0])
    h1, h1b = _merge(o_nsa, o_diff, ga, gb, x2, w_branch_nsa[0].astype(BF), w_branch_diff[0].astype(BF),
                     w_out[0].astype(BF), row(ln1_g[0]), row(ln1_b[0]))
    c, p2, e1, e2 = _route(h1b, peer_wq[0].T.astype(BF), peer_subkeys1[0].astype(BF),
                           peer_subkeys2[0].astype(BF))
    ffn_t = _peer(h1b, peer_u[0].astype(BF), peer_v[0].astype(BF), c, p2, e1, e2)
    out = _final(ffn_t, h1, p[0].reshape(batch * seq, PLE_DIM), ple_w_gate[0].astype(BF),
                 ple_w_proj[0].astype(BF), row(ln2_g[0]), row(ln2_b[0]))
    return out.reshape(batch, seq, d)
```
